```python
import jax, jax.numpy as jnp
from jax import lax
import numpy as np

D_MODEL = 2048
BATCH = 4
SEQ = 8192
DEPTH = 4

N_MIXERS = 3
N_POOL = (DEPTH + 2) // 3
N_DSA = (DEPTH + 1) // 3
N_CONV = DEPTH // 3

POOL_WINDOWS = (2, 4, 8, 16)
POOL_GROUPS = len(POOL_WINDOWS)
POOL_CH = D_MODEL // POOL_GROUPS

HEAD_DIM = 128
N_HEADS = D_MODEL // HEAD_DIM
N_KV_HEADS = 4
IDX_HEADS = 16
IDX_DIM = 64
TOP_K_MAX = 256
QBLK = 128
ROPE_THETA = 10000.0

Q_COLS = N_HEADS * HEAD_DIM
KV_COLS = N_KV_HEADS * HEAD_DIM
IQ_COLS = IDX_HEADS * IDX_DIM
DSA_IN_COLS = Q_COLS + 2 * KV_COLS + IQ_COLS + IDX_DIM + IDX_HEADS
DSA_SPLITS = (Q_COLS, Q_COLS + KV_COLS, Q_COLS + 2 * KV_COLS,
              Q_COLS + 2 * KV_COLS + IQ_COLS, Q_COLS + 2 * KV_COLS + IQ_COLS + IDX_DIM)

CONV_WIDTH = 31
D_FF = 4 * D_MODEL
EPS = 1e-6

kernel_name = "hybrid_pool_dsa_conformer_trunk"


def _rmsnorm(x, g):
    xf = x.astype(jnp.float32)
    y = xf * lax.rsqrt(jnp.mean(xf * xf, axis=-1, keepdims=True) + EPS)
    return (y * g.astype(jnp.float32)).astype(x.dtype)


def _layernorm(x, g, b):
    xf = x.astype(jnp.float32)
    mu = jnp.mean(xf, axis=-1, keepdims=True)
    var = jnp.mean(jnp.square(xf - mu), axis=-1, keepdims=True)
    y = (xf - mu) * lax.rsqrt(var + EPS)
    return (y * g.astype(jnp.float32) + b.astype(jnp.float32)).astype(x.dtype)


def _rope(x, pos):
    d = x.shape[-1]
    inv = ROPE_THETA ** (-jnp.arange(0, d, 2, dtype=jnp.float32) / d)
    ang = pos.astype(jnp.float32)[:, None] * inv[None, :]
    cos = jnp.cos(ang)[None, :, None, :]
    sin = jnp.sin(ang)[None, :, None, :]
    xf = x.astype(jnp.float32)
    x1, x2 = xf[..., : d // 2], xf[..., d // 2:]
    return jnp.concatenate([x1 * cos - x2 * sin, x2 * cos + x1 * sin], axis=-1).astype(x.dtype)


def _pool_mixer(h, w_grp, scale):
    B, L, D = h.shape
    hf = h.astype(jnp.float32)
    pos = jnp.arange(L, dtype=jnp.float32)
    outs = []
    for g, w in enumerate(POOL_WINDOWS):
        xg = hf[..., g * POOL_CH:(g + 1) * POOL_CH]
        cs = jnp.cumsum(xg, axis=1)
        lower = jnp.pad(cs[:, : L - w], ((0, 0), (w, 0), (0, 0)))
        count = jnp.minimum(pos + 1.0, float(w))[None, :, None]
        outs.append((cs - lower) / count - xg)
    y = jnp.stack(outs, axis=2).astype(h.dtype)
    y = jnp.einsum('blgc,gce->blge', y, w_grp).reshape(B, L, D)
    return y * scale


def _dsa_mixer(h, w_in, w_out):
    B, L, _ = h.shape
    pos = jnp.arange(L)
    proj = h @ w_in
    q, k, v, iq, ik, iw = jnp.split(proj, DSA_SPLITS, axis=-1)
    q = _rope(q.reshape(B, L, N_HEADS, HEAD_DIM), pos)
    k = _rope(k.reshape(B, L, N_KV_HEADS, HEAD_DIM), pos)
    v = v.reshape(B, L, N_KV_HEADS, HEAD_DIM)
    iq = _rope(iq.reshape(B, L, IDX_HEADS, IDX_DIM), pos)
    ik = _rope(ik.reshape(B, L, 1, IDX_DIM), pos)[:, :, 0]
    iw = iw * (IDX_HEADS ** -0.5) * (IDX_DIM ** -0.5)
    top_k = min(TOP_K_MAX, L // 4)
    n_blk = L // QBLK
    rep = N_HEADS // N_KV_HEADS
    key_pos = jnp.arange(L)

    def block(bi):
        start = bi * QBLK
        qb = lax.dynamic_slice_in_dim(q, start, QBLK, axis=1)
        iqb = lax.dynamic_slice_in_dim(iq, start, QBLK, axis=1)
        iwb = lax.dynamic_slice_in_dim(iw, start, QBLK, axis=1)
        t = start + jnp.arange(QBLK)
        rel = jax.nn.relu(jnp.einsum('bthd,bsd->bths', iqb, ik))
        score = jnp.einsum('bth,bths->bts', iwb, rel).astype(jnp.float32)
        causal = key_pos[None, :] <= t[:, None]
        score = jnp.where(causal[None], score, -jnp.inf)
        _, idx = lax.top_k(score, top_k)
        valid = idx <= t[None, :, None]
        kg = jax.vmap(lambda kb, ib: kb[ib])(k, idx)
        vg = jax.vmap(lambda vb, ib: vb[ib])(v, idx)
        qg = qb.reshape(B, QBLK, N_KV_HEADS, rep, HEAD_DIM)
        s = jnp.einsum('btgrd,btkgd->btgrk', qg, kg).astype(jnp.float32) * (HEAD_DIM ** -0.5)
        s = jnp.where(valid[:, :, None, None, :], s, -jnp.inf)
        p = jax.nn.softmax(s, axis=-1).astype(vg.dtype)
        o = jnp.einsum('btgrk,btkgd->btgrd', p, vg)
        return o.reshape(B, QBLK, N_HEADS * HEAD_DIM)

    out = lax.map(block, jnp.arange(n_blk))
    out = jnp.transpose(out, (1, 0, 2, 3)).reshape(B, L, N_HEADS * HEAD_DIM)
    return out @ w_out


def _conv_mixer(h, w_pw1, b_pw1, w_dw, b_dw, ln_g, ln_b, w_pw2, b_pw2):
    D = h.shape[-1]
    u = h @ w_pw1 + b_pw1
    a, gate = jnp.split(u, 2, axis=-1)
    u = a * jax.nn.sigmoid(gate)
    u = lax.conv_general_dilated(u, w_dw.reshape(CONV_WIDTH, 1, D), window_strides=(1,),
                                 padding=[(CONV_WIDTH - 1, 0)],
                                 dimension_numbers=('NWC', 'WIO', 'NWC'),
                                 feature_group_count=D) + b_dw
    u = jax.nn.silu(_layernorm(u, ln_g, ln_b))
    return u @ w_pw2 + b_pw2


def setup_inputs(seed: int = 0) -> dict:
    key = jax.random.key(seed)
    ks = jax.random.split(key, 24)
    D = D_MODEL
    f32 = jnp.float32
    nrm = lambda k, shape, s: jax.random.normal(k, shape, f32) * s
    return {
        "x": nrm(ks[0], (BATCH, SEQ, D), 1.0),
        "norm_mix": 1.0 + nrm(ks[1], (DEPTH, D), 0.02),
        "norm_mlp": 1.0 + nrm(ks[2], (DEPTH, D), 0.02),
        "mlp_up": nrm(ks[3], (DEPTH, D, D_FF), D ** -0.5),
        "mlp_down": nrm(ks[4], (DEPTH, D_FF, D), 0.5 * D_FF ** -0.5),
        "pool_w": nrm(ks[5], (N_POOL, POOL_GROUPS, POOL_CH, POOL_CH), POOL_CH ** -0.5),
        "pool_scale": 1.0 + nrm(ks[6], (N_POOL, D), 0.02),
        "dsa_w_in": nrm(ks[7], (N_DSA, D, DSA_IN_COLS), D ** -0.5),
        "dsa_w_out": nrm(ks[8], (N_DSA, Q_COLS, D), Q_COLS ** -0.5),
        "conv_w_pw1": nrm(ks[9], (N_CONV, D, 2 * D), D ** -0.5),
        "conv_b_pw1": nrm(ks[10], (N_CONV, 2 * D), 0.01),
        "conv_w_dw": nrm(ks[11], (N_CONV, CONV_WIDTH, D), CONV_WIDTH ** -0.5),
        "conv_b_dw": nrm(ks[12], (N_CONV, D), 0.01),
        "conv_ln_g": 1.0 + nrm(ks[13], (N_CONV, D), 0.02),
        "conv_ln_b": nrm(ks[14], (N_CONV, D), 0.01),
        "conv_w_pw2": nrm(ks[15], (N_CONV, D, D), D ** -0.5),
        "conv_b_pw2": nrm(ks[16], (N_CONV, D), 0.01),
        "norm_final": 1.0 + nrm(ks[17], (D,), 0.02),
    }


def reference(x, norm_mix, norm_mlp, mlp_up, mlp_down, pool_w, pool_scale, dsa_w_in, dsa_w_out,
              conv_w_pw1, conv_b_pw1, conv_w_dw, conv_b_dw, conv_ln_g, conv_ln_b, conv_w_pw2,
              conv_b_pw2, norm_final):
    for i in range(DEPTH):
        kind, j = i % N_MIXERS, i // N_MIXERS
        h = _rmsnorm(x, norm_mix[i])
        if kind == 0:
            y = _pool_mixer(h, pool_w[j], pool_scale[j])
        elif kind == 1:
            y = _dsa_mixer(h, dsa_w_in[j], dsa_w_out[j])
        else:
            y = _conv_mixer(h, conv_w_pw1[j], conv_b_pw1[j], conv_w_dw[j], conv_b_dw[j],
                            conv_ln_g[j], conv_ln_b[j], conv_w_pw2[j], conv_b_pw2[j])
        x = x + y.astype(x.dtype)
        h = _rmsnorm(x, norm_mlp[i])
        x = x + jnp.square(jax.nn.relu(h @ mlp_up[i])) @ mlp_down[i]
    return _rmsnorm(x, norm_final)
```

```python
import functools

import jax
import jax.numpy as jnp
from jax import lax
from jax.experimental import pallas as pl
from jax.experimental.pallas import tpu as pltpu

F32 = jnp.float32
BF16 = jnp.bfloat16
I32 = jnp.int32

N_MIXERS = 3
POOL_WINDOWS = (2, 4, 8, 16)
HEAD_DIM = 128
N_KV_HEADS = 4
IDX_HEADS = 16
IDX_DIM = 64
TOP_K_MAX = 256
ROPE_THETA = 10000.0
CONV_WIDTH = 31
EPS = 1e-6

LANES = 128
SUBLANES = 8
VMEM_LIMIT_BYTES = 56 * 1024 * 1024

ROW_TILE = 512
FF_TILE = 512
COL_TILE = 512
DSA_TILE = 256
CONV_ROW_TILE = 128
CONV_ROW_CHUNK = 32
POOL_HALO = 16
CONV_HALO = 32

INT_MIN = -(2 ** 31)
MASKED_SCORE = -1e30


def _params(*sem):
    return pltpu.CompilerParams(dimension_semantics=sem, vmem_limit_bytes=VMEM_LIMIT_BYTES)


def _rms(x, g):
    ms = jnp.mean(x * x, axis=-1, keepdims=True)
    return x * lax.rsqrt(ms + EPS) * g


def _dot(a, b):
    return jnp.dot(a, b, preferred_element_type=F32)


def _dot_nt(a, b):
    return lax.dot_general(a, b, (((1,), (1,)), ((), ())), preferred_element_type=F32)


def _mlp_kernel(x_ref, g_ref, up_ref, down_ref, fg_ref, o_ref, h_ref, *, final_norm):
    j = pl.program_id(1)

    @pl.when(j == 0)
    def _():
        x = x_ref[...]
        h_ref[...] = _rms(x, g_ref[...]).astype(BF16)
        o_ref[...] = x

    a = jnp.maximum(_dot(h_ref[...], up_ref[...]), 0.0)
    o_ref[...] += _dot((a * a).astype(BF16), down_ref[...])

    if final_norm:
        @pl.when(j == pl.num_programs(1) - 1)
        def _():
            o_ref[...] = _rms(o_ref[...], fg_ref[...])


def _mlp(x, g, up, down, final_g, final_norm):
    m, d = x.shape
    f = up.shape[1]
    tm, tf = min(ROW_TILE, m), min(FF_TILE, f)
    return pl.pallas_call(
        functools.partial(_mlp_kernel, final_norm=final_norm),
        grid=(m // tm, f // tf),
        in_specs=[
            pl.BlockSpec((tm, d), lambda i, j: (i, 0)),
            pl.BlockSpec((1, d), lambda i, j: (0, 0)),
            pl.BlockSpec((d, tf), lambda i, j: (0, j)),
            pl.BlockSpec((tf, d), lambda i, j: (j, 0)),
            pl.BlockSpec((1, d), lambda i, j: (0, 0)),
        ],
        out_specs=pl.BlockSpec((tm, d), lambda i, j: (i, 0)),
        out_shape=jax.ShapeDtypeStruct((m, d), F32),
        scratch_shapes=[pltpu.VMEM((tm, d), BF16)],
        compiler_params=_params("parallel", "arbitrary"),
        name="mlp",
    )(x, g.reshape(1, d), up, down, final_g.reshape(1, d))


def _pool_kernel(x_ref, xh_ref, g_ref, w_ref, sc_ref, o_ref, ext_ref, *, tiles_per_seq):
    i = pl.program_id(0)
    tm, d = x_ref.shape
    c = d // len(POOL_WINDOWS)
    seq_tile = i % tiles_per_seq
    x = x_ref[...]
    g = g_ref[...]
    h = _rms(x, g)
    ext_ref[0:POOL_HALO, :] = jnp.where(seq_tile == 0, 0.0, _rms(xh_ref[...], g))
    ext_ref[POOL_HALO:, :] = h
    pos = (seq_tile * tm + lax.broadcasted_iota(I32, (tm, 1), 0)).astype(F32)
    for gi, w in enumerate(POOL_WINDOWS):
        cols = slice(gi * c, (gi + 1) * c)
        acc = ext_ref[POOL_HALO:POOL_HALO + tm, cols]
        for j in range(1, w):
            acc = acc + ext_ref[POOL_HALO - j:POOL_HALO - j + tm, cols]
        y = acc / jnp.minimum(pos + 1.0, float(w)) - h[:, cols]
        z = _dot(y.astype(BF16), w_ref[gi])
        o_ref[:, cols] = x[:, cols] + z * sc_ref[:, cols]


def _pool_layer(x, g, w, scale, seq_len):
    m, d = x.shape
    tm = min(ROW_TILE, seq_len)
    ng, c, _ = w.shape
    halo_blocks = tm // POOL_HALO
    return pl.pallas_call(
        functools.partial(_pool_kernel, tiles_per_seq=seq_len // tm),
        grid=(m // tm,),
        in_specs=[
            pl.BlockSpec((tm, d), lambda i: (i, 0)),
            pl.BlockSpec((POOL_HALO, d), lambda i: (jnp.maximum(i * halo_blocks - 1, 0), 0)),
            pl.BlockSpec((1, d), lambda i: (0, 0)),
            pl.BlockSpec((ng, c, c), lambda i: (0, 0, 0)),
            pl.BlockSpec((1, d), lambda i: (0, 0)),
        ],
        out_specs=pl.BlockSpec((tm, d), lambda i: (i, 0)),
        out_shape=jax.ShapeDtypeStruct((m, d), F32),
        scratch_shapes=[pltpu.VMEM((tm + POOL_HALO, d), F32)],
        compiler_params=_params("parallel"),
        name="pool_mixer",
    )(x, x, g.reshape(1, d), w, scale.reshape(1, d))


def _rope_full(x, cos, sin_signed):
    return x * cos + pltpu.roll(x, HEAD_DIM // 2, 1) * sin_signed


def _rope_idx(x, cos, sin_signed, first_half):
    q = IDX_DIM // 2
    rot = jnp.where(first_half, pltpu.roll(x, LANES - q, 1), pltpu.roll(x, q, 1))
    return x * cos + rot * sin_signed


def _proj_kernel(x_ref, g_ref, w_ref, wt_ref, c128_ref, s128_ref, c64_ref, s64_ref,
                 p_ref, ike_ref, iko_ref, iw_ref, h_ref, *, n_rope_blocks, n_main, iw_scale):
    j = pl.program_id(1)
    tm = x_ref.shape[0]
    cb = w_ref.shape[1]
    lane = lax.broadcasted_iota(I32, (tm, LANES), 1)
    first_half = (lane % IDX_DIM) < (IDX_DIM // 2)

    @pl.when(j == 0)
    def _():
        h_ref[...] = _rms(x_ref[...], g_ref[...]).astype(BF16)

    @pl.when(j < n_rope_blocks)
    def _():
        r = _dot(h_ref[...], w_ref[...])
        cos, sin = c128_ref[...], s128_ref[...]
        for c in range(cb // LANES):
            cs = slice(c * LANES, (c + 1) * LANES)
            p_ref[:, cs] = _rope_full(r[:, cs], cos, sin).astype(BF16)

    @pl.when(j == n_rope_blocks)
    def _():
        p_ref[...] = _dot(h_ref[...], w_ref[...]).astype(BF16)

    @pl.when((j > n_rope_blocks) & (j < n_main))
    def _():
        r = _dot(h_ref[...], w_ref[...])
        cos, sin = c64_ref[...], s64_ref[...]
        for c in range(cb // LANES):
            cs = slice(c * LANES, (c + 1) * LANES)
            p_ref[:, cs] = _rope_idx(r[:, cs], cos, sin, first_half).astype(BF16)

    @pl.when(j == n_main)
    def _():
        r = _dot(h_ref[...], wt_ref[...])
        roped = _rope_idx(r, c64_ref[...], s64_ref[...], first_half)
        ike = jnp.where(lane < IDX_DIM, roped, 0.0)
        ike_ref[...] = ike.astype(BF16)
        iko_ref[...] = pltpu.roll(ike, IDX_DIM, 1).astype(BF16)
        iw_ref[...] = r * iw_scale


def _rope_tables(seq_len):
    pos = jnp.arange(seq_len, dtype=F32)[:, None]

    def tables(dim):
        inv = ROPE_THETA ** (-jnp.arange(0, dim, 2, dtype=F32) / dim)
        ang = pos * inv[None, :]
        cos, sin = jnp.cos(ang), jnp.sin(ang)
        reps = LANES // dim
        return (jnp.tile(jnp.concatenate([cos, cos], axis=1), (1, reps)),
                jnp.tile(jnp.concatenate([-sin, sin], axis=1), (1, reps)))

    return tables(HEAD_DIM) + tables(IDX_DIM)


def _dsa_project(x, g, w_in, seq_len):
    m, d = x.shape
    q_cols = d
    kv_cols = N_KV_HEADS * HEAD_DIM
    iq_cols = IDX_HEADS * IDX_DIM
    main_cols = q_cols + 2 * kv_cols + iq_cols
    cb = min(COL_TILE, kv_cols)
    assert kv_cols % cb == 0 and q_cols % cb == 0 and iq_cols % cb == 0 and kv_cols == cb
    n_rope_blocks = (q_cols + kv_cols) // cb
    n_main = main_cols // cb
    assert IDX_DIM + IDX_HEADS <= LANES
    w_main = w_in[:, :main_cols].astype(BF16)
    w_tail = jnp.pad(w_in[:, main_cols:], ((0, 0), (0, LANES - IDX_DIM - IDX_HEADS))).astype(BF16)
    c128, s128, c64, s64 = _rope_tables(seq_len)
    tm = min(ROW_TILE, seq_len)
    tps = seq_len // tm
    tab = pl.BlockSpec((tm, LANES), lambda i, j: (i % tps, 0))
    tail = pl.BlockSpec((tm, LANES), lambda i, j: (i, 0))
    return pl.pallas_call(
        functools.partial(_proj_kernel, n_rope_blocks=n_rope_blocks, n_main=n_main,
                          iw_scale=(IDX_HEADS ** -0.5) * (IDX_DIM ** -0.5)),
        grid=(m // tm, n_main + 1),
        in_specs=[
            pl.BlockSpec((tm, d), lambda i, j: (i, 0)),
            pl.BlockSpec((1, d), lambda i, j: (0, 0)),
            pl.BlockSpec((d, cb), lambda i, j: (0, jnp.minimum(j, n_main - 1))),
            pl.BlockSpec((d, LANES), lambda i, j: (0, 0)),
            tab, tab, tab, tab,
        ],
        out_specs=[
            pl.BlockSpec((tm, cb), lambda i, j: (i, jnp.minimum(j, n_main - 1))),
            tail, tail, tail,
        ],
        out_shape=[
            jax.ShapeDtypeStruct((m, main_cols), BF16),
            jax.ShapeDtypeStruct((m, LANES), BF16),
            jax.ShapeDtypeStruct((m, LANES), BF16),
            jax.ShapeDtypeStruct((m, LANES), F32),
        ],
        scratch_shapes=[pltpu.VMEM((tm, d), BF16)],
        compiler_params=_params("parallel", "arbitrary"),
        name="dsa_project",
    )(x, g.reshape(1, d), w_main, w_tail, c128, s128, c64, s64)


def _dsa_kernel(q_ref, iq_ref, k_ref, v_ref, ike_ref, iko_ref, iw_ref, o_ref,
                key_ref, qst_ref, iqst_ref, wb_ref, m_ref, l_ref, acc_ref, *, top_k, n_heads):
    i = pl.program_id(1)
    t = q_ref.shape[1]
    rep = n_heads // N_KV_HEADS
    n_pairs = IDX_HEADS // 2
    n_chunks = i + 1
    idx_bits = (k_ref.shape[1] - 1).bit_length()

    for h in range(n_heads):
        qst_ref[h * t:(h + 1) * t, :] = q_ref[0, :, h * HEAD_DIM:(h + 1) * HEAD_DIM]
    for p in range(n_pairs):
        iqst_ref[p * t:(p + 1) * t, :] = iq_ref[0, :, p * LANES:(p + 1) * LANES]
    iw = iw_ref[0]
    for h in range(IDX_HEADS):
        wb_ref[h] = jnp.broadcast_to(iw[:, IDX_DIM + h:IDX_DIM + h + 1], (t, t))

    row = i * t + lax.broadcasted_iota(I32, (t, t), 0)
    col_local = lax.broadcasted_iota(I32, (t, t), 1)

    def score_chunk(c, carry):
        s0 = pl.multiple_of(c * t, t)
        iqs = iqst_ref[...]
        re = _dot_nt(iqs, ike_ref[0, pl.ds(s0, t), :])
        ro = _dot_nt(iqs, iko_ref[0, pl.ds(s0, t), :])
        sc = jnp.zeros((t, t), F32)
        for p in range(n_pairs):
            rows = slice(p * t, (p + 1) * t)
            sc = sc + wb_ref[2 * p] * jnp.maximum(re[rows], 0.0)
            sc = sc + wb_ref[2 * p + 1] * jnp.maximum(ro[rows], 0.0)
        bits = lax.bitcast_convert_type(sc, I32)
        key = jnp.where(bits < 0, bits ^ 0x7FFFFFFF, bits)
        key_ref[:, pl.ds(s0, t)] = jnp.where(s0 + col_local <= row, key, INT_MIN)
        return carry

    lax.fori_loop(0, n_chunks, score_chunk, 0)

    lane = lax.broadcasted_iota(I32, (t, LANES), 1)

    def count_where(pred):
        def body(c, cnt):
            s0 = pl.multiple_of(c * t, t)
            for hf in range(t // LANES):
                kk = key_ref[:, pl.ds(s0 + hf * LANES, LANES)]
                cnt = cnt + pred(kk, s0 + hf * LANES + lane).astype(I32)
            return cnt

        cnt = lax.fori_loop(0, n_chunks, body, jnp.zeros((t, LANES), I32))
        return jnp.sum(cnt, axis=1, keepdims=True)

    def count_ge(cand):
        cand_b = jnp.broadcast_to(cand, (t, LANES))
        return count_where(lambda kk, col: kk >= cand_b)

    c0 = count_ge(jnp.zeros((t, 1), I32))
    tau = jnp.where(c0 >= top_k, 0, INT_MIN).astype(I32)
    n_sel = jnp.where(c0 >= top_k, c0, 0)

    def bit_body(bi, carry):
        tau, n_sel = carry
        cand = tau | jnp.left_shift(jnp.int32(1), 30 - bi)
        n = count_ge(cand)
        return jnp.where(n >= top_k, cand, tau), jnp.where(n >= top_k, n, n_sel)

    tau, n_sel = lax.fori_loop(0, 31, bit_body, (tau, n_sel))

    @pl.when(jnp.max(n_sel) > top_k)
    def _():
        need = top_k - count_ge(tau + 1)
        tau_l = jnp.broadcast_to(tau, (t, LANES))

        def idx_body(bi, cut):
            cand = cut | jnp.left_shift(jnp.int32(1), idx_bits - 1 - bi)
            cand_b = jnp.broadcast_to(cand, (t, LANES))
            n = count_where(lambda kk, col: (kk == tau_l) & (col < cand_b))
            return jnp.where(n < need, cand, cut)

        cut = lax.fori_loop(0, idx_bits, idx_body, jnp.zeros((t, 1), I32))
        cut_l = jnp.broadcast_to(jnp.where(n_sel > top_k, cut, k_ref.shape[1]), (t, LANES))

        def demote(c, carry):
            s0 = pl.multiple_of(c * t, t)
            for hf in range(t // LANES):
                cs = pl.ds(s0 + hf * LANES, LANES)
                kk = key_ref[:, cs]
                col = s0 + hf * LANES + lane
                key_ref[:, cs] = jnp.where((kk == tau_l) & (col > cut_l), tau_l - 1, kk)
            return carry

        lax.fori_loop(0, n_chunks, demote, 0)

    tau_b = jnp.broadcast_to(jnp.maximum(tau, INT_MIN + 1), (t, t))

    m_ref[...] = jnp.full(m_ref.shape, MASKED_SCORE, F32)
    l_ref[...] = jnp.zeros(l_ref.shape, F32)
    acc_ref[...] = jnp.zeros(acc_ref.shape, F32)
    scale = HEAD_DIM ** -0.5

    def att_chunk(c, carry):
        s0 = pl.multiple_of(c * t, t)
        neg = jnp.where(key_ref[:, pl.ds(s0, t)] >= tau_b, 0.0, MASKED_SCORE)
        for g in range(N_KV_HEADS):
            gs = slice(g * HEAD_DIM, (g + 1) * HEAD_DIM)
            grows = slice(g * rep * t, (g + 1) * rep * t)
            s = _dot_nt(qst_ref[grows, :], k_ref[0, pl.ds(s0, t), gs]) * scale
            ps = []
            alphas = []
            for r in range(rep):
                hrows = slice((g * rep + r) * t, (g * rep + r + 1) * t)
                sr = s[r * t:(r + 1) * t] + neg
                m_old = m_ref[hrows, :]
                m_new = jnp.maximum(m_old, jnp.max(sr, axis=1, keepdims=True))
                alpha = jnp.exp(m_old - m_new)
                p = jnp.exp(sr - m_new)
                l_ref[hrows, :] = alpha * l_ref[hrows, :] + jnp.sum(p, axis=1, keepdims=True)
                m_ref[hrows, :] = m_new
                ps.append(p.astype(BF16))
                alphas.append(alpha)
            pv = _dot(jnp.concatenate(ps, axis=0), v_ref[0, pl.ds(s0, t), gs])
            for r in range(rep):
                hrows = slice((g * rep + r) * t, (g * rep + r + 1) * t)
                acc_ref[hrows, :] = alphas[r] * acc_ref[hrows, :] + pv[r * t:(r + 1) * t]
        return carry

    lax.fori_loop(0, n_chunks, att_chunk, 0)

    for h in range(n_heads):
        hrows = slice(h * t, (h + 1) * t)
        o_ref[0, :, h * HEAD_DIM:(h + 1) * HEAD_DIM] = (acc_ref[hrows, :] / l_ref[hrows, :]).astype(BF16)


def _dsa_attend(p, ike, iko, iw, batch, seq_len, d):
    n_heads = d // HEAD_DIM
    kv_cols = N_KV_HEADS * HEAD_DIM
    iq_cols = IDX_HEADS * IDX_DIM
    t = min(DSA_TILE, seq_len)
    top_k = min(TOP_K_MAX, seq_len // 4)
    assert (d + 2 * kv_cols) % iq_cols == 0 and d % kv_cols == 0 and t % LANES == 0
    p3 = p.reshape(batch, seq_len, p.shape[-1])
    whole = functools.partial(pl.BlockSpec, pipeline_mode=pl.Buffered(1))
    out = pl.pallas_call(
        functools.partial(_dsa_kernel, top_k=top_k, n_heads=n_heads),
        grid=(batch, seq_len // t),
        in_specs=[
            pl.BlockSpec((1, t, d), lambda b, i: (b, i, 0)),
            pl.BlockSpec((1, t, iq_cols), lambda b, i: (b, i, (d + 2 * kv_cols) // iq_cols)),
            whole((1, seq_len, kv_cols), lambda b, i: (b, 0, d // kv_cols)),
            whole((1, seq_len, kv_cols), lambda b, i: (b, 0, d // kv_cols + 1)),
            whole((1, seq_len, LANES), lambda b, i: (b, 0, 0)),
            whole((1, seq_len, LANES), lambda b, i: (b, 0, 0)),
            pl.BlockSpec((1, t, LANES), lambda b, i: (b, i, 0)),
        ],
        out_specs=pl.BlockSpec((1, t, d), lambda b, i: (b, i, 0)),
        out_shape=jax.ShapeDtypeStruct((batch, seq_len, d), BF16),
        scratch_shapes=[
            pltpu.VMEM((t, seq_len), I32),
            pltpu.VMEM((n_heads * t, HEAD_DIM), BF16),
            pltpu.VMEM((IDX_HEADS // 2 * t, LANES), BF16),
            pltpu.VMEM((IDX_HEADS, t, t), F32),
            pltpu.VMEM((n_heads * t, 1), F32),
            pltpu.VMEM((n_heads * t, 1), F32),
            pltpu.VMEM((n_heads * t, HEAD_DIM), F32),
        ],
        compiler_params=_params("parallel", "arbitrary"),
        name="dsa_attend",
    )(p3, p3, p3, p3,
      ike.reshape(batch, seq_len, LANES), iko.reshape(batch, seq_len, LANES),
      iw.reshape(batch, seq_len, LANES))
    return out.reshape(batch * seq_len, d)


def _proj_res_kernel(a_ref, w_ref, b_ref, x_ref, o_ref):
    o_ref[...] = x_ref[...] + _dot(a_ref[...], w_ref[...]) + b_ref[...]


def _proj_residual(a, w, b, x):
    m, d = x.shape
    kdim = a.shape[1]
    tm = min(ROW_TILE, m)
    return pl.pallas_call(
        _proj_res_kernel,
        grid=(m // tm,),
        in_specs=[
            pl.BlockSpec((tm, kdim), lambda i: (i, 0)),
            pl.BlockSpec((kdim, d), lambda i: (0, 0)),
            pl.BlockSpec((1, d), lambda i: (0, 0)),
            pl.BlockSpec((tm, d), lambda i: (i, 0)),
        ],
        out_specs=pl.BlockSpec((tm, d), lambda i: (i, 0)),
        out_shape=jax.ShapeDtypeStruct((m, d), F32),
        compiler_params=_params("parallel"),
        name="proj_residual",
    )(a, w, b.reshape(1, d), x)


def _glu_kernel(x_ref, g_ref, wa_ref, wg_ref, ba_ref, bg_ref, o_ref, h_ref):
    @pl.when(pl.program_id(1) == 0)
    def _():
        h_ref[...] = _rms(x_ref[...], g_ref[...]).astype(BF16)

    h = h_ref[...]
    a = _dot(h, wa_ref[...]) + ba_ref[...]
    gate = _dot(h, wg_ref[...]) + bg_ref[...]
    o_ref[...] = a * jax.nn.sigmoid(gate)


def _glu(x, g, w, b):
    m, d = x.shape
    tm, tn = min(ROW_TILE, m), min(COL_TILE, d)
    nb = d // tn
    b2 = b.reshape(1, 2 * d)
    return pl.pallas_call(
        _glu_kernel,
        grid=(m // tm, nb),
        in_specs=[
            pl.BlockSpec((tm, d), lambda i, j: (i, 0)),
            pl.BlockSpec((1, d), lambda i, j: (0, 0)),
            pl.BlockSpec((d, tn), lambda i, j: (0, j)),
            pl.BlockSpec((d, tn), lambda i, j: (0, j + nb)),
            pl.BlockSpec((1, tn), lambda i, j: (0, j)),
            pl.BlockSpec((1, tn), lambda i, j: (0, j + nb)),
        ],
        out_specs=pl.BlockSpec((tm, tn), lambda i, j: (i, j)),
        out_shape=jax.ShapeDtypeStruct((m, d), F32),
        scratch_shapes=[pltpu.VMEM((tm, d), BF16)],
        compiler_params=_params("parallel", "arbitrary"),
        name="conv_glu",
    )(x, g.reshape(1, d), w, w, b2, b2)


def _dwconv_kernel(u_ref, uh_ref, w_ref, b_ref, lg_ref, lb_ref, o_ref, ext_ref, c_ref, *, tiles_per_seq):
    i = pl.program_id(0)
    tm, d = u_ref.shape
    ext_ref[0:CONV_HALO, :] = jnp.where(i % tiles_per_seq == 0, 0.0, uh_ref[...])
    ext_ref[CONV_HALO:, :] = u_ref[...]
    off = CONV_HALO - (CONV_WIDTH - 1)
    cw = min(COL_TILE, d)
    rc = min(CONV_ROW_CHUNK, tm)
    for c0 in range(0, d, cw):
        cols = slice(c0, c0 + cw)
        for r0 in range(0, tm, rc):
            acc = jnp.zeros((rc, cw), F32)
            for k in range(CONV_WIDTH):
                acc = acc + ext_ref[off + k + r0:off + k + r0 + rc, cols] * w_ref[k:k + 1, cols]
            c_ref[r0:r0 + rc, cols] = acc + b_ref[:, cols]
    c = c_ref[...]
    mu = jnp.mean(c, axis=-1, keepdims=True)
    cc = c - mu
    var = jnp.mean(cc * cc, axis=-1, keepdims=True)
    y = cc * lax.rsqrt(var + EPS) * lg_ref[...] + lb_ref[...]
    o_ref[...] = (y * jax.nn.sigmoid(y)).astype(BF16)


def _dwconv(u, w_dw, b_dw, ln_g, ln_b, seq_len):
    m, d = u.shape
    tm = min(CONV_ROW_TILE, seq_len)
    halo_blocks = tm // CONV_HALO
    row = lambda v: v.reshape(1, d)
    return pl.pallas_call(
        functools.partial(_dwconv_kernel, tiles_per_seq=seq_len // tm),
        grid=(m // tm,),
        in_specs=[
            pl.BlockSpec((tm, d), lambda i: (i, 0)),
            pl.BlockSpec((CONV_HALO, d), lambda i: (jnp.maximum(i * halo_blocks - 1, 0), 0)),
            pl.BlockSpec((CONV_WIDTH, d), lambda i: (0, 0)),
            pl.BlockSpec((1, d), lambda i: (0, 0)),
            pl.BlockSpec((1, d), lambda i: (0, 0)),
            pl.BlockSpec((1, d), lambda i: (0, 0)),
        ],
        out_specs=pl.BlockSpec((tm, d), lambda i: (i, 0)),
        out_shape=jax.ShapeDtypeStruct((m, d), BF16),
        scratch_shapes=[pltpu.VMEM((tm + CONV_HALO, d), F32), pltpu.VMEM((tm, d), F32)],
        compiler_params=_params("parallel"),
        name="conv_dw_ln",
    )(u, u, w_dw, row(b_dw), row(ln_g), row(ln_b))


def kernel(x, norm_mix, norm_mlp, mlp_up, mlp_down, pool_w, pool_scale, dsa_w_in, dsa_w_out,
           conv_w_pw1, conv_b_pw1, conv_w_dw, conv_b_dw, conv_ln_g, conv_ln_b, conv_w_pw2,
           conv_b_pw2, norm_final):
    batch, seq_len, d = x.shape
    depth = norm_mix.shape[0]
    xf = x.reshape(batch * seq_len, d)
    for i in range(depth):
        kind, j = i % N_MIXERS, i // N_MIXERS
        if kind == 0:
            xf = _pool_layer(xf, norm_mix[i], pool_w[j].astype(BF16), pool_scale[j], seq_len)
        elif kind == 1:
            p, ike, iko, iw = _dsa_project(xf, norm_mix[i], dsa_w_in[j], seq_len)
            o = _dsa_attend(p, ike, iko, iw, batch, seq_len, d)
            xf = _proj_residual(o, dsa_w_out[j].astype(BF16), jnp.zeros((d,), F32), xf)
        else:
            u = _glu(xf, norm_mix[i], conv_w_pw1[j].astype(BF16), conv_b_pw1[j])
            u = _dwconv(u, conv_w_dw[j], conv_b_dw[j], conv_ln_g[j], conv_ln_b[j], seq_len)
            xf = _proj_residual(u, conv_w_pw2[j].astype(BF16), conv_b_pw2[j], xf)
        xf = _mlp(xf, norm_mlp[i], mlp_up[i].astype(BF16), mlp_down[i].astype(BF16),
                  norm_final, final_norm=(i == depth - 1))
    return xf.reshape(batch, seq_len, d)
```

```python
import functools

import jax
import jax.numpy as jnp
from jax import lax
from jax.experimental import pallas as pl
from jax.experimental.pallas import tpu as pltpu

F32 = jnp.float32
BF16 = jnp.bfloat16
I32 = jnp.int32

N_MIXERS = 3
POOL_WINDOWS = (2, 4, 8, 16)
HEAD_DIM = 128
N_KV_HEADS = 4
IDX_HEADS = 16
IDX_DIM = 64
TOP_K_MAX = 256
ROPE_THETA = 10000.0
CONV_WIDTH = 31
EPS = 1e-6

LANES = 128
SUBLANES = 8
VMEM_LIMIT_BYTES = 56 * 1024 * 1024

ROW_TILE = 512
MLP_ROW_TILE = 1024
FF_TILE = 512
COL_TILE = 512
DSA_TILE = 256
DSA_KEY_BLOCK = 512
CONV_ROW_TILE = 128
CONV_ROW_CHUNK = 32
POOL_HALO = 16
CONV_HALO = 32

INT_MIN = -(2 ** 31)
MASKED_SCORE = -1e30


def _params(*sem):
    return pltpu.CompilerParams(dimension_semantics=sem, vmem_limit_bytes=VMEM_LIMIT_BYTES)


def _rms(x, g):
    ms = jnp.mean(x * x, axis=-1, keepdims=True)
    return x * lax.rsqrt(ms + EPS) * g


def _dot(a, b):
    return jnp.dot(a, b, preferred_element_type=F32)


def _dot_nt(a, b):
    return lax.dot_general(a, b, (((1,), (1,)), ((), ())), preferred_element_type=F32)


def _mlp_kernel(x_ref, g_ref, up_ref, down_ref, fg_ref, o_ref, h_ref, *, final_norm):
    j = pl.program_id(1)

    @pl.when(j == 0)
    def _():
        x = x_ref[...]
        h_ref[...] = _rms(x, g_ref[...]).astype(BF16)
        o_ref[...] = x

    a = jnp.maximum(_dot(h_ref[...], up_ref[...]), 0.0)
    o_ref[...] += _dot((a * a).astype(BF16), down_ref[...])

    if final_norm:
        @pl.when(j == pl.num_programs(1) - 1)
        def _():
            o_ref[...] = _rms(o_ref[...], fg_ref[...])


def _mlp(x, g, up, down, final_g, final_norm):
    m, d = x.shape
    f = up.shape[1]
    tm, tf = min(MLP_ROW_TILE, m), min(FF_TILE, f)
    return pl.pallas_call(
        functools.partial(_mlp_kernel, final_norm=final_norm),
        grid=(m // tm, f // tf),
        in_specs=[
            pl.BlockSpec((tm, d), lambda i, j: (i, 0)),
            pl.BlockSpec((1, d), lambda i, j: (0, 0)),
            pl.BlockSpec((d, tf), lambda i, j: (0, j)),
            pl.BlockSpec((tf, d), lambda i, j: (j, 0)),
            pl.BlockSpec((1, d), lambda i, j: (0, 0)),
        ],
        out_specs=pl.BlockSpec((tm, d), lambda i, j: (i, 0)),
        out_shape=jax.ShapeDtypeStruct((m, d), F32),
        scratch_shapes=[pltpu.VMEM((tm, d), BF16)],
        compiler_params=_params("parallel", "arbitrary"),
        name="mlp",
    )(x, g.reshape(1, d), up, down, final_g.reshape(1, d))


def _pool_kernel(x_ref, xh_ref, g_ref, w_ref, sc_ref, o_ref, ext_ref, *, tiles_per_seq):
    i = pl.program_id(0)
    tm, d = x_ref.shape
    c = d // len(POOL_WINDOWS)
    seq_tile = i % tiles_per_seq
    x = x_ref[...]
    g = g_ref[...]
    h = _rms(x, g)
    ext_ref[0:POOL_HALO, :] = jnp.where(seq_tile == 0, 0.0, _rms(xh_ref[...], g))
    ext_ref[POOL_HALO:, :] = h
    pos = (seq_tile * tm + lax.broadcasted_iota(I32, (tm, 1), 0)).astype(F32)
    for gi, w in enumerate(POOL_WINDOWS):
        cols = slice(gi * c, (gi + 1) * c)
        acc = ext_ref[POOL_HALO:POOL_HALO + tm, cols]
        for j in range(1, w):
            acc = acc + ext_ref[POOL_HALO - j:POOL_HALO - j + tm, cols]
        y = acc / jnp.minimum(pos + 1.0, float(w)) - h[:, cols]
        z = _dot(y.astype(BF16), w_ref[gi])
        o_ref[:, cols] = x[:, cols] + z * sc_ref[:, cols]


def _pool_layer(x, g, w, scale, seq_len):
    m, d = x.shape
    tm = min(ROW_TILE, seq_len)
    ng, c, _ = w.shape
    halo_blocks = tm // POOL_HALO
    return pl.pallas_call(
        functools.partial(_pool_kernel, tiles_per_seq=seq_len // tm),
        grid=(m // tm,),
        in_specs=[
            pl.BlockSpec((tm, d), lambda i: (i, 0)),
            pl.BlockSpec((POOL_HALO, d), lambda i: (jnp.maximum(i * halo_blocks - 1, 0), 0)),
            pl.BlockSpec((1, d), lambda i: (0, 0)),
            pl.BlockSpec((ng, c, c), lambda i: (0, 0, 0)),
            pl.BlockSpec((1, d), lambda i: (0, 0)),
        ],
        out_specs=pl.BlockSpec((tm, d), lambda i: (i, 0)),
        out_shape=jax.ShapeDtypeStruct((m, d), F32),
        scratch_shapes=[pltpu.VMEM((tm + POOL_HALO, d), F32)],
        compiler_params=_params("parallel"),
        name="pool_mixer",
    )(x, x, g.reshape(1, d), w, scale.reshape(1, d))


def _rope_full(x, cos, sin_signed):
    return x * cos + pltpu.roll(x, HEAD_DIM // 2, 1) * sin_signed


def _rope_idx(x, cos, sin_signed, first_half):
    q = IDX_DIM // 2
    rot = jnp.where(first_half, pltpu.roll(x, LANES - q, 1), pltpu.roll(x, q, 1))
    return x * cos + rot * sin_signed


def _proj_kernel(x_ref, g_ref, w_ref, wt_ref, c128_ref, s128_ref, c64_ref, s64_ref,
                 p_ref, ike_ref, iko_ref, iw_ref, h_ref, *, n_rope_blocks, n_main, iw_scale):
    j = pl.program_id(1)
    tm = x_ref.shape[0]
    cb = w_ref.shape[1]
    lane = lax.broadcasted_iota(I32, (tm, LANES), 1)
    first_half = (lane % IDX_DIM) < (IDX_DIM // 2)

    @pl.when(j == 0)
    def _():
        h_ref[...] = _rms(x_ref[...], g_ref[...]).astype(BF16)

    @pl.when(j < n_rope_blocks)
    def _():
        r = _dot(h_ref[...], w_ref[...])
        cos, sin = c128_ref[...], s128_ref[...]
        for c in range(cb // LANES):
            cs = slice(c * LANES, (c + 1) * LANES)
            p_ref[:, cs] = _rope_full(r[:, cs], cos, sin).astype(BF16)

    @pl.when(j == n_rope_blocks)
    def _():
        p_ref[...] = _dot(h_ref[...], w_ref[...]).astype(BF16)

    @pl.when((j > n_rope_blocks) & (j < n_main))
    def _():
        r = _dot(h_ref[...], w_ref[...])
        cos, sin = c64_ref[...], s64_ref[...]
        for c in range(cb // LANES):
            cs = slice(c * LANES, (c + 1) * LANES)
            p_ref[:, cs] = _rope_idx(r[:, cs], cos, sin, first_half).astype(BF16)

    @pl.when(j == n_main)
    def _():
        r = _dot(h_ref[...], wt_ref[...])
        roped = _rope_idx(r, c64_ref[...], s64_ref[...], first_half)
        ike = jnp.where(lane < IDX_DIM, roped, 0.0)
        ike_ref[...] = ike.astype(BF16)
        iko_ref[...] = pltpu.roll(ike, IDX_DIM, 1).astype(BF16)
        iw_ref[...] = r * iw_scale


def _rope_tables(seq_len):
    pos = jnp.arange(seq_len, dtype=F32)[:, None]

    def tables(dim):
        inv = ROPE_THETA ** (-jnp.arange(0, dim, 2, dtype=F32) / dim)
        ang = pos * inv[None, :]
        cos, sin = jnp.cos(ang), jnp.sin(ang)
        reps = LANES // dim
        return (jnp.tile(jnp.concatenate([cos, cos], axis=1), (1, reps)),
                jnp.tile(jnp.concatenate([-sin, sin], axis=1), (1, reps)))

    return tables(HEAD_DIM) + tables(IDX_DIM)


def _dsa_project(x, g, w_in, seq_len):
    m, d = x.shape
    q_cols = d
    kv_cols = N_KV_HEADS * HEAD_DIM
    iq_cols = IDX_HEADS * IDX_DIM
    main_cols = q_cols + 2 * kv_cols + iq_cols
    cb = min(COL_TILE, kv_cols)
    assert kv_cols % cb == 0 and q_cols % cb == 0 and iq_cols % cb == 0 and kv_cols == cb
    n_rope_blocks = (q_cols + kv_cols) // cb
    n_main = main_cols // cb
    assert IDX_DIM + IDX_HEADS <= LANES
    w_main = w_in[:, :main_cols].astype(BF16)
    w_tail = jnp.pad(w_in[:, main_cols:], ((0, 0), (0, LANES - IDX_DIM - IDX_HEADS))).astype(BF16)
    c128, s128, c64, s64 = _rope_tables(seq_len)
    tm = min(ROW_TILE, seq_len)
    tps = seq_len // tm
    tab = pl.BlockSpec((tm, LANES), lambda i, j: (i % tps, 0))
    tail = pl.BlockSpec((tm, LANES), lambda i, j: (i, 0))
    return pl.pallas_call(
        functools.partial(_proj_kernel, n_rope_blocks=n_rope_blocks, n_main=n_main,
                          iw_scale=(IDX_HEADS ** -0.5) * (IDX_DIM ** -0.5)),
        grid=(m // tm, n_main + 1),
        in_specs=[
            pl.BlockSpec((tm, d), lambda i, j: (i, 0)),
            pl.BlockSpec((1, d), lambda i, j: (0, 0)),
            pl.BlockSpec((d, cb), lambda i, j: (0, jnp.minimum(j, n_main - 1))),
            pl.BlockSpec((d, LANES), lambda i, j: (0, 0)),
            tab, tab, tab, tab,
        ],
        out_specs=[
            pl.BlockSpec((tm, cb), lambda i, j: (i, jnp.minimum(j, n_main - 1))),
            tail, tail, tail,
        ],
        out_shape=[
            jax.ShapeDtypeStruct((m, main_cols), BF16),
            jax.ShapeDtypeStruct((m, LANES), BF16),
            jax.ShapeDtypeStruct((m, LANES), BF16),
            jax.ShapeDtypeStruct((m, LANES), F32),
        ],
        scratch_shapes=[pltpu.VMEM((tm, d), BF16)],
        compiler_params=_params("parallel", "arbitrary"),
        name="dsa_project",
    )(x, g.reshape(1, d), w_main, w_tail, c128, s128, c64, s64)


def _dsa_kernel(q_ref, iq_ref, k_ref, v_ref, ike_ref, iko_ref, iw_ref, o_ref,
                keyt_ref, qst_ref, iwt_ref, m_ref, acc_ref, *, top_k, n_heads, kb):
    i = pl.program_id(1)
    t = q_ref.shape[1]
    seq_len = k_ref.shape[1]
    rep = n_heads // N_KV_HEADS
    n_blocks = (i * t) // kb + 1
    idx_bits = (seq_len - 1).bit_length()

    for h in range(n_heads):
        qst_ref[h * t:(h + 1) * t, :] = q_ref[0, :, h * HEAD_DIM:(h + 1) * HEAD_DIM]
    iwt_ref[...] = iw_ref[0].T

    krow_local = lax.broadcasted_iota(I32, (kb, t), 0)
    qpos = i * t + lax.broadcasted_iota(I32, (1, t), 1)

    def score_block(b, carry):
        s0 = pl.multiple_of(b * kb, kb)
        ike = ike_ref[0, pl.ds(s0, kb), :]
        iko = iko_ref[0, pl.ds(s0, kb), :]
        sc = jnp.zeros((kb, t), F32)
        for p in range(IDX_HEADS // 2):
            iqp = iq_ref[0, :, p * LANES:(p + 1) * LANES]
            r0 = IDX_DIM + 2 * p
            sc = sc + iwt_ref[r0:r0 + 1, :] * jnp.maximum(_dot_nt(ike, iqp), 0.0)
            sc = sc + iwt_ref[r0 + 1:r0 + 2, :] * jnp.maximum(_dot_nt(iko, iqp), 0.0)
        bits = lax.bitcast_convert_type(sc, I32)
        key = jnp.where(bits < 0, bits ^ 0x7FFFFFFF, bits)
        keyt_ref[pl.ds(s0, kb), :] = jnp.where(s0 + krow_local <= qpos, key, INT_MIN)
        return carry

    lax.fori_loop(0, n_blocks, score_block, 0)

    def count_where(pred):
        def body(b, cnt):
            s0 = pl.multiple_of(b * kb, kb)
            hit = pred(keyt_ref[pl.ds(s0, kb), :], s0 + krow_local).astype(I32)
            return cnt + jnp.sum(hit.reshape(kb // SUBLANES, SUBLANES, t), axis=0)

        cnt = lax.fori_loop(0, n_blocks, body, jnp.zeros((SUBLANES, t), I32))
        return jnp.sum(cnt, axis=0, keepdims=True)

    def count_ge(cand):
        return count_where(lambda kk, krow: kk >= cand)

    c0 = count_ge(jnp.zeros((1, t), I32))
    tau = jnp.where(c0 >= top_k, 0, INT_MIN).astype(I32)
    n_sel = jnp.where(c0 >= top_k, c0, 0)

    def n_pending(n_sel):
        return jnp.max(jnp.where((qpos >= top_k) & (n_sel != top_k), 1, 0))

    def bit_cond(c):
        return (c[0] < 31) & (c[3] > 0)

    def bit_body(c):
        bi, tau, n_sel, _ = c
        cand = tau | jnp.left_shift(jnp.int32(1), 30 - bi)
        n = count_ge(cand)
        tau = jnp.where(n >= top_k, cand, tau)
        n_sel = jnp.where(n >= top_k, n, n_sel)
        return bi + 1, tau, n_sel, n_pending(n_sel)

    _, tau, n_sel, _ = lax.while_loop(bit_cond, bit_body, (jnp.int32(0), tau, n_sel, n_pending(n_sel)))

    @pl.when(jnp.max(n_sel) > top_k)
    def _():
        need = top_k - count_ge(tau + 1)

        def idx_body(bi, cut):
            cand = cut | jnp.left_shift(jnp.int32(1), idx_bits - 1 - bi)
            n = count_where(lambda kk, krow: (kk == tau) & (krow < cand))
            return jnp.where(n < need, cand, cut)

        cut = lax.fori_loop(0, idx_bits, idx_body, jnp.zeros((1, t), I32))
        cut = jnp.where(n_sel > top_k, cut, seq_len)

        def demote(b, carry):
            s0 = pl.multiple_of(b * kb, kb)
            kk = keyt_ref[pl.ds(s0, kb), :]
            keyt_ref[pl.ds(s0, kb), :] = jnp.where((kk == tau) & (s0 + krow_local > cut), tau - 1, kk)
            return carry

        lax.fori_loop(0, n_blocks, demote, 0)

    tau = jnp.maximum(tau, INT_MIN + 1)

    m_ref[...] = jnp.full(m_ref.shape, MASKED_SCORE, F32)
    acc_ref[...] = jnp.zeros(acc_ref.shape, F32)
    scale = HEAD_DIM ** -0.5
    ones = jnp.ones((kb, HEAD_DIM), BF16)

    def att_block(b, carry):
        s0 = pl.multiple_of(b * kb, kb)
        neg = jnp.where(keyt_ref[pl.ds(s0, kb), :] >= tau, 0.0, MASKED_SCORE).T
        for g in range(N_KV_HEADS):
            gs = slice(g * HEAD_DIM, (g + 1) * HEAD_DIM)
            grows = slice(g * rep * t, (g + 1) * rep * t)
            s = _dot_nt(qst_ref[grows, :], k_ref[0, pl.ds(s0, kb), gs]) * scale
            v1 = jnp.concatenate([v_ref[0, pl.ds(s0, kb), gs], ones], axis=1)
            ps = []
            alphas = []
            for r in range(rep):
                hrows = slice((g * rep + r) * t, (g * rep + r + 1) * t)
                sr = s[r * t:(r + 1) * t] + neg
                tiles = [sr[:, c * LANES:(c + 1) * LANES] for c in range(kb // LANES)]
                mx = functools.reduce(jnp.maximum, tiles)
                m_old = m_ref[hrows, :]
                m_new = jnp.maximum(m_old, jnp.max(mx, axis=1, keepdims=True))
                m_ref[hrows, :] = m_new
                alphas.append(jnp.exp(m_old - m_new))
                ps.append(jnp.concatenate([jnp.exp(x - m_new) for x in tiles], axis=1).astype(BF16))
            pv = _dot(jnp.concatenate(ps, axis=0), v1)
            for r in range(rep):
                hrows = slice((g * rep + r) * t, (g * rep + r + 1) * t)
                for c in range(2):
                    cs = slice(c * HEAD_DIM, (c + 1) * HEAD_DIM)
                    acc_ref[hrows, cs] = alphas[r] * acc_ref[hrows, cs] + pv[r * t:(r + 1) * t, cs]
        return carry

    lax.fori_loop(0, n_blocks, att_block, 0)

    for h in range(n_heads):
        hrows = slice(h * t, (h + 1) * t)
        o = acc_ref[hrows, 0:HEAD_DIM] / acc_ref[hrows, HEAD_DIM:2 * HEAD_DIM]
        o_ref[0, :, h * HEAD_DIM:(h + 1) * HEAD_DIM] = o.astype(BF16)


def _dsa_attend(p, ike, iko, iw, batch, seq_len, d):
    n_heads = d // HEAD_DIM
    kv_cols = N_KV_HEADS * HEAD_DIM
    iq_cols = IDX_HEADS * IDX_DIM
    t = min(DSA_TILE, seq_len)
    kb = min(DSA_KEY_BLOCK, seq_len)
    top_k = min(TOP_K_MAX, seq_len // 4)
    assert (d + 2 * kv_cols) % iq_cols == 0 and d % kv_cols == 0 and t % LANES == 0 and kb % t == 0
    p3 = p.reshape(batch, seq_len, p.shape[-1])
    whole = functools.partial(pl.BlockSpec, pipeline_mode=pl.Buffered(1))
    out = pl.pallas_call(
        functools.partial(_dsa_kernel, top_k=top_k, n_heads=n_heads, kb=kb),
        grid=(batch, seq_len // t),
        in_specs=[
            pl.BlockSpec((1, t, d), lambda b, i: (b, i, 0)),
            pl.BlockSpec((1, t, iq_cols), lambda b, i: (b, i, (d + 2 * kv_cols) // iq_cols)),
            whole((1, seq_len, kv_cols), lambda b, i: (b, 0, d // kv_cols)),
            whole((1, seq_len, kv_cols), lambda b, i: (b, 0, d // kv_cols + 1)),
            whole((1, seq_len, LANES), lambda b, i: (b, 0, 0)),
            whole((1, seq_len, LANES), lambda b, i: (b, 0, 0)),
            pl.BlockSpec((1, t, LANES), lambda b, i: (b, i, 0)),
        ],
        out_specs=pl.BlockSpec((1, t, d), lambda b, i: (b, i, 0)),
        out_shape=jax.ShapeDtypeStruct((batch, seq_len, d), BF16),
        scratch_shapes=[
            pltpu.VMEM((seq_len, t), I32),
            pltpu.VMEM((n_heads * t, HEAD_DIM), BF16),
            pltpu.VMEM((LANES, t), F32),
            pltpu.VMEM((n_heads * t, LANES), F32),
            pltpu.VMEM((n_heads * t, 2 * HEAD_DIM), F32),
        ],
        compiler_params=_params("parallel", "arbitrary"),
        name="dsa_attend",
    )(p3, p3, p3, p3,
      ike.reshape(batch, seq_len, LANES), iko.reshape(batch, seq_len, LANES),
      iw.reshape(batch, seq_len, LANES))
    return out.reshape(batch * seq_len, d)


def _proj_res_kernel(a_ref, w_ref, b_ref, x_ref, o_ref):
    o_ref[...] = x_ref[...] + _dot(a_ref[...], w_ref[...]) + b_ref[...]


def _proj_residual(a, w, b, x):
    m, d = x.shape
    kdim = a.shape[1]
    tm = min(ROW_TILE, m)
    return pl.pallas_call(
        _proj_res_kernel,
        grid=(m // tm,),
        in_specs=[
            pl.BlockSpec((tm, kdim), lambda i: (i, 0)),
            pl.BlockSpec((kdim, d), lambda i: (0, 0)),
            pl.BlockSpec((1, d), lambda i: (0, 0)),
            pl.BlockSpec((tm, d), lambda i: (i, 0)),
        ],
        out_specs=pl.BlockSpec((tm, d), lambda i: (i, 0)),
        out_shape=jax.ShapeDtypeStruct((m, d), F32),
        compiler_params=_params("parallel"),
        name="proj_residual",
    )(a, w, b.reshape(1, d), x)


def _glu_kernel(x_ref, g_ref, wa_ref, wg_ref, ba_ref, bg_ref, o_ref, h_ref):
    @pl.when(pl.program_id(1) == 0)
    def _():
        h_ref[...] = _rms(x_ref[...], g_ref[...]).astype(BF16)

    h = h_ref[...]
    a = _dot(h, wa_ref[...]) + ba_ref[...]
    gate = _dot(h, wg_ref[...]) + bg_ref[...]
    o_ref[...] = a * jax.nn.sigmoid(gate)


def _glu(x, g, w, b):
    m, d = x.shape
    tm, tn = min(ROW_TILE, m), min(COL_TILE, d)
    nb = d // tn
    b2 = b.reshape(1, 2 * d)
    return pl.pallas_call(
        _glu_kernel,
        grid=(m // tm, nb),
        in_specs=[
            pl.BlockSpec((tm, d), lambda i, j: (i, 0)),
            pl.BlockSpec((1, d), lambda i, j: (0, 0)),
            pl.BlockSpec((d, tn), lambda i, j: (0, j)),
            pl.BlockSpec((d, tn), lambda i, j: (0, j + nb)),
            pl.BlockSpec((1, tn), lambda i, j: (0, j)),
            pl.BlockSpec((1, tn), lambda i, j: (0, j + nb)),
        ],
        out_specs=pl.BlockSpec((tm, tn), lambda i, j: (i, j)),
        out_shape=jax.ShapeDtypeStruct((m, d), F32),
        scratch_shapes=[pltpu.VMEM((tm, d), BF16)],
        compiler_params=_params("parallel", "arbitrary"),
        name="conv_glu",
    )(x, g.reshape(1, d), w, w, b2, b2)


def _dwconv_kernel(u_ref, uh_ref, w_ref, b_ref, lg_ref, lb_ref, o_ref, sh_ref, c_ref, *, tiles_per_seq):
    i = pl.program_id(0)
    tm, d = u_ref.shape
    rows = tm + CONV_HALO
    halo = jnp.where(i % tiles_per_seq == 0, 0.0, uh_ref[...])
    off = CONV_HALO - (CONV_WIDTH - 1)
    cw = sh_ref.shape[2]
    rc = min(CONV_ROW_CHUNK, tm)
    for c0 in range(0, d, cw):
        cols = slice(c0, c0 + cw)
        ext = jnp.concatenate([halo[:, cols], u_ref[:, cols]], axis=0)
        sh_ref[0] = ext
        for j in range(1, SUBLANES):
            sh_ref[j] = pltpu.roll(ext, rows - j, 0)
        for r0 in range(0, tm, rc):
            acc = jnp.zeros((rc, cw), F32)
            for k in range(CONV_WIDTH):
                j = (off + k) % SUBLANES
                a = off + k - j + r0
                acc = acc + sh_ref[j, a:a + rc, :] * w_ref[k:k + 1, cols]
            c_ref[r0:r0 + rc, cols] = acc + b_ref[:, cols]
    c = c_ref[...]
    mu = jnp.mean(c, axis=-1, keepdims=True)
    cc = c - mu
    var = jnp.mean(cc * cc, axis=-1, keepdims=True)
    y = cc * lax.rsqrt(var + EPS) * lg_ref[...] + lb_ref[...]
    o_ref[...] = (y * jax.nn.sigmoid(y)).astype(BF16)


def _dwconv(u, w_dw, b_dw, ln_g, ln_b, seq_len):
    m, d = u.shape
    tm = min(CONV_ROW_TILE, seq_len)
    halo_blocks = tm // CONV_HALO
    row = lambda v: v.reshape(1, d)
    return pl.pallas_call(
        functools.partial(_dwconv_kernel, tiles_per_seq=seq_len // tm),
        grid=(m // tm,),
        in_specs=[
            pl.BlockSpec((tm, d), lambda i: (i, 0)),
            pl.BlockSpec((CONV_HALO, d), lambda i: (jnp.maximum(i * halo_blocks - 1, 0), 0)),
            pl.BlockSpec((CONV_WIDTH, d), lambda i: (0, 0)),
            pl.BlockSpec((1, d), lambda i: (0, 0)),
            pl.BlockSpec((1, d), lambda i: (0, 0)),
            pl.BlockSpec((1, d), lambda i: (0, 0)),
        ],
        out_specs=pl.BlockSpec((tm, d), lambda i: (i, 0)),
        out_shape=jax.ShapeDtypeStruct((m, d), BF16),
        scratch_shapes=[pltpu.VMEM((SUBLANES, tm + CONV_HALO, min(COL_TILE, d)), F32), pltpu.VMEM((tm, d), F32)],
        compiler_params=_params("parallel"),
        name="conv_dw_ln",
    )(u, u, w_dw, row(b_dw), row(ln_g), row(ln_b))


def kernel(x, norm_mix, norm_mlp, mlp_up, mlp_down, pool_w, pool_scale, dsa_w_in, dsa_w_out,
           conv_w_pw1, conv_b_pw1, conv_w_dw, conv_b_dw, conv_ln_g, conv_ln_b, conv_w_pw2,
           conv_b_pw2, norm_final):
    batch, seq_len, d = x.shape
    depth = norm_mix.shape[0]
    xf = x.reshape(batch * seq_len, d)
    for i in range(depth):
        kind, j = i % N_MIXERS, i // N_MIXERS
        if kind == 0:
            xf = _pool_layer(xf, norm_mix[i], pool_w[j].astype(BF16), pool_scale[j], seq_len)
        elif kind == 1:
            p, ike, iko, iw = _dsa_project(xf, norm_mix[i], dsa_w_in[j], seq_len)
            o = _dsa_attend(p, ike, iko, iw, batch, seq_len, d)
            xf = _proj_residual(o, dsa_w_out[j].astype(BF16), jnp.zeros((d,), F32), xf)
        else:
            u = _glu(xf, norm_mix[i], conv_w_pw1[j].astype(BF16), conv_b_pw1[j])
            u = _dwconv(u, conv_w_dw[j], conv_b_dw[j], conv_ln_g[j], conv_ln_b[j], seq_len)
            xf = _proj_residual(u, conv_w_pw2[j].astype(BF16), conv_b_pw2[j], xf)
        xf = _mlp(xf, norm_mlp[i], mlp_up[i].astype(BF16), mlp_down[i].astype(BF16),
                  norm_final, final_norm=(i == depth - 1))
    return xf.reshape(batch, seq_len, d)
```

```python
import functools

import jax
import jax.numpy as jnp
from jax import lax
from jax.experimental import pallas as pl
from jax.experimental.pallas import tpu as pltpu

F32 = jnp.float32
BF16 = jnp.bfloat16
I32 = jnp.int32

N_MIXERS = 3
POOL_WINDOWS = (2, 4, 8, 16)
HEAD_DIM = 128
N_KV_HEADS = 4
IDX_HEADS = 16
IDX_DIM = 64
TOP_K_MAX = 256
ROPE_THETA = 10000.0
CONV_WIDTH = 31
EPS = 1e-6

LANES = 128
SUBLANES = 8
VMEM_LIMIT_BYTES = 56 * 1024 * 1024

ROW_TILE = 512
MLP_ROW_TILE = 1024
FF_TILE = 512
COL_TILE = 512
DSA_TILE = 256
DSA_KEY_BLOCK = 512
COUNT_CHAINS = 8
BIT_STEPS_PER_CHECK = 2
CONV_ROW_TILE = 128
CONV_ROW_CHUNK = 32
POOL_HALO = 16
CONV_HALO = 32

INT_MIN = -(2 ** 31)
MASKED_SCORE = -1e30
LOG2_E = 1.4426950408889634


def _params(*sem):
    return pltpu.CompilerParams(dimension_semantics=sem, vmem_limit_bytes=VMEM_LIMIT_BYTES)


def _rms(x, g):
    ms = jnp.mean(x * x, axis=-1, keepdims=True)
    return x * lax.rsqrt(ms + EPS) * g


def _dot(a, b):
    return jnp.dot(a, b, preferred_element_type=F32)


def _dot_nt(a, b):
    return lax.dot_general(a, b, (((1,), (1,)), ((), ())), preferred_element_type=F32)


def _mlp_kernel(x_ref, g_ref, up_ref, down_ref, fg_ref, o_ref, h_ref, *, final_norm):
    j = pl.program_id(1)

    @pl.when(j == 0)
    def _():
        x = x_ref[...]
        h_ref[...] = _rms(x, g_ref[...]).astype(BF16)
        o_ref[...] = x

    a = jnp.maximum(_dot(h_ref[...], up_ref[...]), 0.0)
    o_ref[...] += _dot((a * a).astype(BF16), down_ref[...])

    if final_norm:
        @pl.when(j == pl.num_programs(1) - 1)
        def _():
            o_ref[...] = _rms(o_ref[...], fg_ref[...])


def _mlp(x, g, up, down, final_g, final_norm):
    m, d = x.shape
    f = up.shape[1]
    tm, tf = min(MLP_ROW_TILE, m), min(FF_TILE, f)
    return pl.pallas_call(
        functools.partial(_mlp_kernel, final_norm=final_norm),
        grid=(m // tm, f // tf),
        in_specs=[
            pl.BlockSpec((tm, d), lambda i, j: (i, 0)),
            pl.BlockSpec((1, d), lambda i, j: (0, 0)),
            pl.BlockSpec((d, tf), lambda i, j: (0, j)),
            pl.BlockSpec((tf, d), lambda i, j: (j, 0)),
            pl.BlockSpec((1, d), lambda i, j: (0, 0)),
        ],
        out_specs=pl.BlockSpec((tm, d), lambda i, j: (i, 0)),
        out_shape=jax.ShapeDtypeStruct((m, d), F32),
        scratch_shapes=[pltpu.VMEM((tm, d), BF16)],
        compiler_params=_params("parallel", "arbitrary"),
        name="mlp",
    )(x, g.reshape(1, d), up, down, final_g.reshape(1, d))


def _pool_kernel(x_ref, xh_ref, g_ref, w_ref, sc_ref, o_ref, ext_ref, *, tiles_per_seq):
    i = pl.program_id(0)
    tm, d = x_ref.shape
    c = d // len(POOL_WINDOWS)
    seq_tile = i % tiles_per_seq
    x = x_ref[...]
    g = g_ref[...]
    h = _rms(x, g)
    ext_ref[0:POOL_HALO, :] = jnp.where(seq_tile == 0, 0.0, _rms(xh_ref[...], g))
    ext_ref[POOL_HALO:, :] = h
    pos = (seq_tile * tm + lax.broadcasted_iota(I32, (tm, 1), 0)).astype(F32)
    for gi, w in enumerate(POOL_WINDOWS):
        cols = slice(gi * c, (gi + 1) * c)
        acc = ext_ref[POOL_HALO:POOL_HALO + tm, cols]
        for j in range(1, w):
            acc = acc + ext_ref[POOL_HALO - j:POOL_HALO - j + tm, cols]
        y = acc / jnp.minimum(pos + 1.0, float(w)) - h[:, cols]
        z = _dot(y.astype(BF16), w_ref[gi])
        o_ref[:, cols] = x[:, cols] + z * sc_ref[:, cols]


def _pool_layer(x, g, w, scale, seq_len):
    m, d = x.shape
    tm = min(ROW_TILE, seq_len)
    ng, c, _ = w.shape
    halo_blocks = tm // POOL_HALO
    return pl.pallas_call(
        functools.partial(_pool_kernel, tiles_per_seq=seq_len // tm),
        grid=(m // tm,),
        in_specs=[
            pl.BlockSpec((tm, d), lambda i: (i, 0)),
            pl.BlockSpec((POOL_HALO, d), lambda i: (jnp.maximum(i * halo_blocks - 1, 0), 0)),
            pl.BlockSpec((1, d), lambda i: (0, 0)),
            pl.BlockSpec((ng, c, c), lambda i: (0, 0, 0)),
            pl.BlockSpec((1, d), lambda i: (0, 0)),
        ],
        out_specs=pl.BlockSpec((tm, d), lambda i: (i, 0)),
        out_shape=jax.ShapeDtypeStruct((m, d), F32),
        scratch_shapes=[pltpu.VMEM((tm + POOL_HALO, d), F32)],
        compiler_params=_params("parallel"),
        name="pool_mixer",
    )(x, x, g.reshape(1, d), w, scale.reshape(1, d))


def _rope_full(x, cos, sin_signed):
    return x * cos + pltpu.roll(x, HEAD_DIM // 2, 1) * sin_signed


def _rope_idx(x, cos, sin_signed, first_half):
    q = IDX_DIM // 2
    rot = jnp.where(first_half, pltpu.roll(x, LANES - q, 1), pltpu.roll(x, q, 1))
    return x * cos + rot * sin_signed


def _proj_kernel(x_ref, g_ref, w_ref, wt_ref, c128_ref, s128_ref, c64_ref, s64_ref,
                 p_ref, ike_ref, iko_ref, iw_ref, h_ref, *, n_q_blocks, n_rope_blocks, n_main, q_scale, iw_scale):
    j = pl.program_id(1)
    tm = x_ref.shape[0]
    cb = w_ref.shape[1]
    lane = lax.broadcasted_iota(I32, (tm, LANES), 1)
    first_half = (lane % IDX_DIM) < (IDX_DIM // 2)

    @pl.when(j == 0)
    def _():
        h_ref[...] = _rms(x_ref[...], g_ref[...]).astype(BF16)

    @pl.when(j < n_rope_blocks)
    def _():
        r = _dot(h_ref[...], w_ref[...]) * jnp.where(j < n_q_blocks, q_scale, 1.0)
        cos, sin = c128_ref[...], s128_ref[...]
        for c in range(cb // LANES):
            cs = slice(c * LANES, (c + 1) * LANES)
            p_ref[:, cs] = _rope_full(r[:, cs], cos, sin).astype(BF16)

    @pl.when(j == n_rope_blocks)
    def _():
        p_ref[...] = _dot(h_ref[...], w_ref[...]).astype(BF16)

    @pl.when((j > n_rope_blocks) & (j < n_main))
    def _():
        r = _dot(h_ref[...], w_ref[...])
        cos, sin = c64_ref[...], s64_ref[...]
        for c in range(cb // LANES):
            cs = slice(c * LANES, (c + 1) * LANES)
            p_ref[:, cs] = _rope_idx(r[:, cs], cos, sin, first_half).astype(BF16)

    @pl.when(j == n_main)
    def _():
        r = _dot(h_ref[...], wt_ref[...])
        roped = _rope_idx(r, c64_ref[...], s64_ref[...], first_half)
        ike = jnp.where(lane < IDX_DIM, roped, 0.0)
        ike_ref[...] = ike.astype(BF16)
        iko_ref[...] = pltpu.roll(ike, IDX_DIM, 1).astype(BF16)
        iw_ref[...] = r * iw_scale


def _rope_tables(seq_len):
    pos = jnp.arange(seq_len, dtype=F32)[:, None]

    def tables(dim):
        inv = ROPE_THETA ** (-jnp.arange(0, dim, 2, dtype=F32) / dim)
        ang = pos * inv[None, :]
        cos, sin = jnp.cos(ang), jnp.sin(ang)
        reps = LANES // dim
        return (jnp.tile(jnp.concatenate([cos, cos], axis=1), (1, reps)),
                jnp.tile(jnp.concatenate([-sin, sin], axis=1), (1, reps)))

    return tables(HEAD_DIM) + tables(IDX_DIM)


def _dsa_project(x, g, w_in, seq_len):
    m, d = x.shape
    q_cols = d
    kv_cols = N_KV_HEADS * HEAD_DIM
    iq_cols = IDX_HEADS * IDX_DIM
    main_cols = q_cols + 2 * kv_cols + iq_cols
    cb = min(COL_TILE, kv_cols)
    assert kv_cols % cb == 0 and q_cols % cb == 0 and iq_cols % cb == 0 and kv_cols == cb
    n_rope_blocks = (q_cols + kv_cols) // cb
    n_main = main_cols // cb
    assert IDX_DIM + IDX_HEADS <= LANES
    w_main = w_in[:, :main_cols].astype(BF16)
    w_tail = jnp.pad(w_in[:, main_cols:], ((0, 0), (0, LANES - IDX_DIM - IDX_HEADS))).astype(BF16)
    c128, s128, c64, s64 = _rope_tables(seq_len)
    tm = min(MLP_ROW_TILE, seq_len)
    tps = seq_len // tm
    tab = pl.BlockSpec((tm, LANES), lambda i, j: (i % tps, 0))
    tail = pl.BlockSpec((tm, LANES), lambda i, j: (i, 0))
    return pl.pallas_call(
        functools.partial(_proj_kernel, n_q_blocks=q_cols // cb, n_rope_blocks=n_rope_blocks, n_main=n_main,
                          q_scale=(HEAD_DIM ** -0.5) * LOG2_E, iw_scale=(IDX_HEADS ** -0.5) * (IDX_DIM ** -0.5)),
        grid=(m // tm, n_main + 1),
        in_specs=[
            pl.BlockSpec((tm, d), lambda i, j: (i, 0)),
            pl.BlockSpec((1, d), lambda i, j: (0, 0)),
            pl.BlockSpec((d, cb), lambda i, j: (0, jnp.minimum(j, n_main - 1))),
            pl.BlockSpec((d, LANES), lambda i, j: (0, 0)),
            tab, tab, tab, tab,
        ],
        out_specs=[
            pl.BlockSpec((tm, cb), lambda i, j: (i, jnp.minimum(j, n_main - 1))),
            tail, tail, tail,
        ],
        out_shape=[
            jax.ShapeDtypeStruct((m, main_cols), BF16),
            jax.ShapeDtypeStruct((m, LANES), BF16),
            jax.ShapeDtypeStruct((m, LANES), BF16),
            jax.ShapeDtypeStruct((m, LANES), F32),
        ],
        scratch_shapes=[pltpu.VMEM((tm, d), BF16)],
        compiler_params=_params("parallel", "arbitrary"),
        name="dsa_project",
    )(x, g.reshape(1, d), w_main, w_tail, c128, s128, c64, s64)


def _dsa_kernel(q_ref, iq_ref, k_ref, v_ref, ike_ref, iko_ref, iw_ref, o_ref,
                keyt_ref, qst_ref, iwt_ref, m_ref, acc_ref, *, top_k, n_heads, kb):
    i = pl.program_id(1)
    t = q_ref.shape[1]
    seq_len = k_ref.shape[1]
    rep = n_heads // N_KV_HEADS
    n_blocks = (i * t) // kb + 1
    idx_bits = (seq_len - 1).bit_length()

    for h in range(n_heads):
        qst_ref[h * t:(h + 1) * t, :] = q_ref[0, :, h * HEAD_DIM:(h + 1) * HEAD_DIM]
    iwt_ref[...] = iw_ref[0].T

    krow_local = lax.broadcasted_iota(I32, (kb, t), 0)
    qpos = i * t + lax.broadcasted_iota(I32, (1, t), 1)

    def score_block(b, carry):
        s0 = pl.multiple_of(b * kb, kb)
        ike = ike_ref[0, pl.ds(s0, kb), :]
        iko = iko_ref[0, pl.ds(s0, kb), :]
        sc = jnp.zeros((kb, t), F32)
        for p in range(IDX_HEADS // 2):
            iqp = iq_ref[0, :, p * LANES:(p + 1) * LANES]
            r0 = IDX_DIM + 2 * p
            sc = sc + iwt_ref[r0:r0 + 1, :] * jnp.maximum(_dot_nt(ike, iqp), 0.0)
            sc = sc + iwt_ref[r0 + 1:r0 + 2, :] * jnp.maximum(_dot_nt(iko, iqp), 0.0)
        bits = lax.bitcast_convert_type(sc, I32)
        key = jnp.where(bits < 0, bits ^ 0x7FFFFFFF, bits)
        keyt_ref[pl.ds(s0, kb), :] = jnp.where(s0 + krow_local <= qpos, key, INT_MIN)
        return carry

    lax.fori_loop(0, n_blocks, score_block, 0)

    def count_where(pred):
        def body(b, cnt):
            s0 = pl.multiple_of(b * kb, kb)
            hit = pred(keyt_ref[pl.ds(s0, kb), :], s0 + krow_local).astype(I32)
            part = jnp.sum(hit.reshape(COUNT_CHAINS, kb // (COUNT_CHAINS * SUBLANES), SUBLANES, t), axis=1)
            return cnt + jnp.sum(part, axis=0)

        cnt = lax.fori_loop(0, n_blocks, body, jnp.zeros((SUBLANES, t), I32))
        return jnp.sum(cnt, axis=0, keepdims=True)

    def count_ge(cand):
        return count_where(lambda kk, krow: kk >= cand)

    c0 = count_ge(jnp.zeros((1, t), I32))
    tau = jnp.where(c0 >= top_k, 0, INT_MIN).astype(I32)
    n_sel = jnp.where(c0 >= top_k, c0, 0)

    def n_pending(n_sel):
        return jnp.max(jnp.where((qpos >= top_k) & (n_sel != top_k), 1, 0))

    def bit_cond(c):
        return (c[0] < 31) & (c[3] > 0)

    def bit_body(c):
        bi, tau, n_sel, _ = c
        for _ in range(BIT_STEPS_PER_CHECK):
            cand = tau | jnp.left_shift(jnp.int32(1), jnp.maximum(30 - bi, 0))
            n = count_ge(cand)
            keep = (n >= top_k) & (bi < 31)
            tau = jnp.where(keep, cand, tau)
            n_sel = jnp.where(keep, n, n_sel)
            bi = bi + 1
        return bi, tau, n_sel, n_pending(n_sel)

    _, tau, n_sel, _ = lax.while_loop(bit_cond, bit_body, (jnp.int32(0), tau, n_sel, n_pending(n_sel)))

    @pl.when(jnp.max(n_sel) > top_k)
    def _():
        need = top_k - count_ge(tau + 1)

        def idx_body(bi, cut):
            cand = cut | jnp.left_shift(jnp.int32(1), idx_bits - 1 - bi)
            n = count_where(lambda kk, krow: (kk == tau) & (krow < cand))
            return jnp.where(n < need, cand, cut)

        cut = lax.fori_loop(0, idx_bits, idx_body, jnp.zeros((1, t), I32))
        cut = jnp.where(n_sel > top_k, cut, seq_len)

        def demote(b, carry):
            s0 = pl.multiple_of(b * kb, kb)
            kk = keyt_ref[pl.ds(s0, kb), :]
            keyt_ref[pl.ds(s0, kb), :] = jnp.where((kk == tau) & (s0 + krow_local > cut), tau - 1, kk)
            return carry

        lax.fori_loop(0, n_blocks, demote, 0)

    tau = jnp.maximum(tau, INT_MIN + 1)

    m_ref[...] = jnp.full(m_ref.shape, MASKED_SCORE, F32)
    acc_ref[...] = jnp.zeros(acc_ref.shape, F32)
    ones = jnp.ones((kb, HEAD_DIM), BF16)

    def att_block(b, carry):
        s0 = pl.multiple_of(b * kb, kb)
        neg = jnp.where(keyt_ref[pl.ds(s0, kb), :] >= tau, 0.0, MASKED_SCORE).T
        for g in range(N_KV_HEADS):
            gs = slice(g * HEAD_DIM, (g + 1) * HEAD_DIM)
            grows = slice(g * rep * t, (g + 1) * rep * t)
            s = _dot_nt(qst_ref[grows, :], k_ref[0, pl.ds(s0, kb), gs])
            v1 = jnp.concatenate([v_ref[0, pl.ds(s0, kb), gs], ones], axis=1)
            ps = []
            alphas = []
            for r in range(rep):
                hrows = slice((g * rep + r) * t, (g * rep + r + 1) * t)
                sr = s[r * t:(r + 1) * t] + neg
                tiles = [sr[:, c * LANES:(c + 1) * LANES] for c in range(kb // LANES)]
                mx = functools.reduce(jnp.maximum, tiles)
                m_old = m_ref[hrows, :]
                m_new = jnp.maximum(m_old, jnp.max(mx, axis=1, keepdims=True))
                m_ref[hrows, :] = m_new
                alphas.append(jnp.exp2(m_old - m_new))
                ps.append(jnp.concatenate([jnp.exp2(x - m_new) for x in tiles], axis=1).astype(BF16))
            pv = _dot(jnp.concatenate(ps, axis=0), v1)
            for r in range(rep):
                hrows = slice((g * rep + r) * t, (g * rep + r + 1) * t)
                for c in range(2):
                    cs = slice(c * HEAD_DIM, (c + 1) * HEAD_DIM)
                    acc_ref[hrows, cs] = alphas[r] * acc_ref[hrows, cs] + pv[r * t:(r + 1) * t, cs]
        return carry

    lax.fori_loop(0, n_blocks, att_block, 0)

    for h in range(n_heads):
        hrows = slice(h * t, (h + 1) * t)
        o = acc_ref[hrows, 0:HEAD_DIM] / acc_ref[hrows, HEAD_DIM:2 * HEAD_DIM]
        o_ref[0, :, h * HEAD_DIM:(h + 1) * HEAD_DIM] = o.astype(BF16)


def _dsa_attend(p, ike, iko, iw, batch, seq_len, d):
    n_heads = d // HEAD_DIM
    kv_cols = N_KV_HEADS * HEAD_DIM
    iq_cols = IDX_HEADS * IDX_DIM
    t = min(DSA_TILE, seq_len)
    kb = min(DSA_KEY_BLOCK, seq_len)
    top_k = min(TOP_K_MAX, seq_len // 4)
    assert (d + 2 * kv_cols) % iq_cols == 0 and d % kv_cols == 0 and t % LANES == 0 and kb % t == 0
    p3 = p.reshape(batch, seq_len, p.shape[-1])
    whole = functools.partial(pl.BlockSpec, pipeline_mode=pl.Buffered(1))
    out = pl.pallas_call(
        functools.partial(_dsa_kernel, top_k=top_k, n_heads=n_heads, kb=kb),
        grid=(batch, seq_len // t),
        in_specs=[
            pl.BlockSpec((1, t, d), lambda b, i: (b, i, 0)),
            pl.BlockSpec((1, t, iq_cols), lambda b, i: (b, i, (d + 2 * kv_cols) // iq_cols)),
            whole((1, seq_len, kv_cols), lambda b, i: (b, 0, d // kv_cols)),
            whole((1, seq_len, kv_cols), lambda b, i: (b, 0, d // kv_cols + 1)),
            whole((1, seq_len, LANES), lambda b, i: (b, 0, 0)),
            whole((1, seq_len, LANES), lambda b, i: (b, 0, 0)),
            pl.BlockSpec((1, t, LANES), lambda b, i: (b, i, 0)),
        ],
        out_specs=pl.BlockSpec((1, t, d), lambda b, i: (b, i, 0)),
        out_shape=jax.ShapeDtypeStruct((batch, seq_len, d), BF16),
        scratch_shapes=[
            pltpu.VMEM((seq_len, t), I32),
            pltpu.VMEM((n_heads * t, HEAD_DIM), BF16),
            pltpu.VMEM((LANES, t), F32),
            pltpu.VMEM((n_heads * t, LANES), F32),
            pltpu.VMEM((n_heads * t, 2 * HEAD_DIM), F32),
        ],
        compiler_params=_params("parallel", "arbitrary"),
        name="dsa_attend",
    )(p3, p3, p3, p3,
      ike.reshape(batch, seq_len, LANES), iko.reshape(batch, seq_len, LANES),
      iw.reshape(batch, seq_len, LANES))
    return out.reshape(batch * seq_len, d)


def _proj_res_kernel(a_ref, w_ref, b_ref, x_ref, o_ref):
    o_ref[...] = x_ref[...] + _dot(a_ref[...], w_ref[...]) + b_ref[...]


def _proj_residual(a, w, b, x):
    m, d = x.shape
    kdim = a.shape[1]
    tm = min(ROW_TILE, m)
    return pl.pallas_call(
        _proj_res_kernel,
        grid=(m // tm,),
        in_specs=[
            pl.BlockSpec((tm, kdim), lambda i: (i, 0)),
            pl.BlockSpec((kdim, d), lambda i: (0, 0)),
            pl.BlockSpec((1, d), lambda i: (0, 0)),
            pl.BlockSpec((tm, d), lambda i: (i, 0)),
        ],
        out_specs=pl.BlockSpec((tm, d), lambda i: (i, 0)),
        out_shape=jax.ShapeDtypeStruct((m, d), F32),
        compiler_params=_params("parallel"),
        name="proj_residual",
    )(a, w, b.reshape(1, d), x)


def _glu_kernel(x_ref, g_ref, wa_ref, wg_ref, ba_ref, bg_ref, o_ref, h_ref):
    @pl.when(pl.program_id(1) == 0)
    def _():
        h_ref[...] = _rms(x_ref[...], g_ref[...]).astype(BF16)

    h = h_ref[...]
    a = _dot(h, wa_ref[...]) + ba_ref[...]
    gate = _dot(h, wg_ref[...]) + bg_ref[...]
    o_ref[...] = a * jax.nn.sigmoid(gate)


def _glu(x, g, w, b):
    m, d = x.shape
    tm, tn = min(MLP_ROW_TILE, m), min(COL_TILE, d)
    nb = d // tn
    b2 = b.reshape(1, 2 * d)
    return pl.pallas_call(
        _glu_kernel,
        grid=(m // tm, nb),
        in_specs=[
            pl.BlockSpec((tm, d), lambda i, j: (i, 0)),
            pl.BlockSpec((1, d), lambda i, j: (0, 0)),
            pl.BlockSpec((d, tn), lambda i, j: (0, j)),
            pl.BlockSpec((d, tn), lambda i, j: (0, j + nb)),
            pl.BlockSpec((1, tn), lambda i, j: (0, j)),
            pl.BlockSpec((1, tn), lambda i, j: (0, j + nb)),
        ],
        out_specs=pl.BlockSpec((tm, tn), lambda i, j: (i, j)),
        out_shape=jax.ShapeDtypeStruct((m, d), F32),
        scratch_shapes=[pltpu.VMEM((tm, d), BF16)],
        compiler_params=_params("parallel", "arbitrary"),
        name="conv_glu",
    )(x, g.reshape(1, d), w, w, b2, b2)


def _dwconv_kernel(u_ref, uh_ref, w_ref, b_ref, lg_ref, lb_ref, o_ref, sh_ref, c_ref, *, tiles_per_seq):
    i = pl.program_id(0)
    tm, d = u_ref.shape
    rows = tm + CONV_HALO
    halo = jnp.where(i % tiles_per_seq == 0, 0.0, uh_ref[...])
    off = CONV_HALO - (CONV_WIDTH - 1)
    cw = sh_ref.shape[2]
    rc = min(CONV_ROW_CHUNK, tm)
    for c0 in range(0, d, cw):
        cols = slice(c0, c0 + cw)
        ext = jnp.concatenate([halo[:, cols], u_ref[:, cols]], axis=0)
        sh_ref[0] = ext
        for j in range(1, SUBLANES):
            sh_ref[j] = pltpu.roll(ext, rows - j, 0)
        for r0 in range(0, tm, rc):
            acc = jnp.zeros((rc, cw), F32)
            for k in range(CONV_WIDTH):
                j = (off + k) % SUBLANES
                a = off + k - j + r0
                acc = acc + sh_ref[j, a:a + rc, :] * w_ref[k:k + 1, cols]
            c_ref[r0:r0 + rc, cols] = acc + b_ref[:, cols]
    c = c_ref[...]
    mu = jnp.mean(c, axis=-1, keepdims=True)
    cc = c - mu
    var = jnp.mean(cc * cc, axis=-1, keepdims=True)
    y = cc * lax.rsqrt(var + EPS) * lg_ref[...] + lb_ref[...]
    o_ref[...] = (y * jax.nn.sigmoid(y)).astype(BF16)


def _dwconv(u, w_dw, b_dw, ln_g, ln_b, seq_len):
    m, d = u.shape
    tm = min(CONV_ROW_TILE, seq_len)
    halo_blocks = tm // CONV_HALO
    row = lambda v: v.reshape(1, d)
    return pl.pallas_call(
        functools.partial(_dwconv_kernel, tiles_per_seq=seq_len // tm),
        grid=(m // tm,),
        in_specs=[
            pl.BlockSpec((tm, d), lambda i: (i, 0)),
            pl.BlockSpec((CONV_HALO, d), lambda i: (jnp.maximum(i * halo_blocks - 1, 0), 0)),
            pl.BlockSpec((CONV_WIDTH, d), lambda i: (0, 0)),
            pl.BlockSpec((1, d), lambda i: (0, 0)),
            pl.BlockSpec((1, d), lambda i: (0, 0)),
            pl.BlockSpec((1, d), lambda i: (0, 0)),
        ],
        out_specs=pl.BlockSpec((tm, d), lambda i: (i, 0)),
        out_shape=jax.ShapeDtypeStruct((m, d), BF16),
        scratch_shapes=[pltpu.VMEM((SUBLANES, tm + CONV_HALO, min(COL_TILE, d)), F32), pltpu.VMEM((tm, d), F32)],
        compiler_params=_params("parallel"),
        name="conv_dw_ln",
    )(u, u, w_dw, row(b_dw), row(ln_g), row(ln_b))


def kernel(x, norm_mix, norm_mlp, mlp_up, mlp_down, pool_w, pool_scale, dsa_w_in, dsa_w_out,
           conv_w_pw1, conv_b_pw1, conv_w_dw, conv_b_dw, conv_ln_g, conv_ln_b, conv_w_pw2,
           conv_b_pw2, norm_final):
    batch, seq_len, d = x.shape
    depth = norm_mix.shape[0]
    xf = x.reshape(batch * seq_len, d)
    for i in range(depth):
        kind, j = i % N_MIXERS, i // N_MIXERS
        if kind == 0:
            xf = _pool_layer(xf, norm_mix[i], pool_w[j].astype(BF16), pool_scale[j], seq_len)
        elif kind == 1:
            p, ike, iko, iw = _dsa_project(xf, norm_mix[i], dsa_w_in[j], seq_len)
            o = _dsa_attend(p, ike, iko, iw, batch, seq_len, d)
            xf = _proj_residual(o, dsa_w_out[j].astype(BF16), jnp.zeros((d,), F32), xf)
        else:
            u = _glu(xf, norm_mix[i], conv_w_pw1[j].astype(BF16), conv_b_pw1[j])
            u = _dwconv(u, conv_w_dw[j], conv_b_dw[j], conv_ln_g[j], conv_ln_b[j], seq_len)
            xf = _proj_residual(u, conv_w_pw2[j].astype(BF16), conv_b_pw2[j], xf)
        xf = _mlp(xf, norm_mlp[i], mlp_up[i].astype(BF16), mlp_down[i].astype(BF16),
                  norm_final, final_norm=(i == depth - 1))
    return xf.reshape(batch, seq_len, d)
```

```python
import functools

import jax
import jax.numpy as jnp
from jax import lax
from jax.experimental import pallas as pl
from jax.experimental.pallas import tpu as pltpu

F32 = jnp.float32
BF16 = jnp.bfloat16
I32 = jnp.int32
I16 = jnp.int16

N_MIXERS = 3
POOL_WINDOWS = (2, 4, 8, 16)
HEAD_DIM = 128
N_KV_HEADS = 4
IDX_HEADS = 16
IDX_DIM = 64
TOP_K_MAX = 256
ROPE_THETA = 10000.0
CONV_WIDTH = 31
EPS = 1e-6

LANES = 128
SUBLANES = 8
VMEM_LIMIT_BYTES = 56 * 1024 * 1024

ROW_TILE = 512
MLP_ROW_TILE = 1024
FF_TILE = 512
COL_TILE = 512
DSA_TILE = 256
DSA_KEY_BLOCK = 512
COUNT_CHAINS = 8
BIT_STEPS_PER_CHECK = 4
HI_BIT_STEPS = 15
CONV_ROW_TILE = 128
CONV_ROW_CHUNK = 32
POOL_HALO = 16
CONV_HALO = 32

INT_MIN = -(2 ** 31)
MASKED_SCORE = -1e30
LOG2_E = 1.4426950408889634


def _params(*sem):
    return pltpu.CompilerParams(dimension_semantics=sem, vmem_limit_bytes=VMEM_LIMIT_BYTES)


def _rms(x, g):
    ms = jnp.mean(x * x, axis=-1, keepdims=True)
    return x * lax.rsqrt(ms + EPS) * g


def _dot(a, b):
    return jnp.dot(a, b, preferred_element_type=F32)


def _dot_nt(a, b):
    return lax.dot_general(a, b, (((1,), (1,)), ((), ())), preferred_element_type=F32)


def _mlp_kernel(x_ref, g_ref, up_ref, down_ref, fg_ref, o_ref, h_ref, *, final_norm):
    j = pl.program_id(1)

    @pl.when(j == 0)
    def _():
        x = x_ref[...]
        h_ref[...] = _rms(x, g_ref[...]).astype(BF16)
        o_ref[...] = x

    a = jnp.maximum(_dot(h_ref[...], up_ref[...]), 0.0)
    o_ref[...] += _dot((a * a).astype(BF16), down_ref[...])

    if final_norm:
        @pl.when(j == pl.num_programs(1) - 1)
        def _():
            o_ref[...] = _rms(o_ref[...], fg_ref[...])


def _mlp(x, g, up, down, final_g, final_norm):
    m, d = x.shape
    f = up.shape[1]
    tm, tf = min(MLP_ROW_TILE, m), min(FF_TILE, f)
    return pl.pallas_call(
        functools.partial(_mlp_kernel, final_norm=final_norm),
        grid=(m // tm, f // tf),
        in_specs=[
            pl.BlockSpec((tm, d), lambda i, j: (i, 0)),
            pl.BlockSpec((1, d), lambda i, j: (0, 0)),
            pl.BlockSpec((d, tf), lambda i, j: (0, j)),
            pl.BlockSpec((tf, d), lambda i, j: (j, 0)),
            pl.BlockSpec((1, d), lambda i, j: (0, 0)),
        ],
        out_specs=pl.BlockSpec((tm, d), lambda i, j: (i, 0)),
        out_shape=jax.ShapeDtypeStruct((m, d), F32),
        scratch_shapes=[pltpu.VMEM((tm, d), BF16)],
        compiler_params=_params("parallel", "arbitrary"),
        name="mlp",
    )(x, g.reshape(1, d), up, down, final_g.reshape(1, d))


def _pool_kernel(x_ref, xh_ref, g_ref, w_ref, sc_ref, o_ref, ext_ref, *, tiles_per_seq):
    i = pl.program_id(0)
    tm, d = x_ref.shape
    c = d // len(POOL_WINDOWS)
    seq_tile = i % tiles_per_seq
    x = x_ref[...]
    g = g_ref[...]
    h = _rms(x, g)
    ext_ref[0:POOL_HALO, :] = jnp.where(seq_tile == 0, 0.0, _rms(xh_ref[...], g))
    ext_ref[POOL_HALO:, :] = h
    pos = (seq_tile * tm + lax.broadcasted_iota(I32, (tm, 1), 0)).astype(F32)
    for gi, w in enumerate(POOL_WINDOWS):
        cols = slice(gi * c, (gi + 1) * c)
        acc = ext_ref[POOL_HALO:POOL_HALO + tm, cols]
        for j in range(1, w):
            acc = acc + ext_ref[POOL_HALO - j:POOL_HALO - j + tm, cols]
        y = acc / jnp.minimum(pos + 1.0, float(w)) - h[:, cols]
        z = _dot(y.astype(BF16), w_ref[gi])
        o_ref[:, cols] = x[:, cols] + z * sc_ref[:, cols]


def _pool_layer(x, g, w, scale, seq_len):
    m, d = x.shape
    tm = min(ROW_TILE, seq_len)
    ng, c, _ = w.shape
    halo_blocks = tm // POOL_HALO
    return pl.pallas_call(
        functools.partial(_pool_kernel, tiles_per_seq=seq_len // tm),
        grid=(m // tm,),
        in_specs=[
            pl.BlockSpec((tm, d), lambda i: (i, 0)),
            pl.BlockSpec((POOL_HALO, d), lambda i: (jnp.maximum(i * halo_blocks - 1, 0), 0)),
            pl.BlockSpec((1, d), lambda i: (0, 0)),
            pl.BlockSpec((ng, c, c), lambda i: (0, 0, 0)),
            pl.BlockSpec((1, d), lambda i: (0, 0)),
        ],
        out_specs=pl.BlockSpec((tm, d), lambda i: (i, 0)),
        out_shape=jax.ShapeDtypeStruct((m, d), F32),
        scratch_shapes=[pltpu.VMEM((tm + POOL_HALO, d), F32)],
        compiler_params=_params("parallel"),
        name="pool_mixer",
    )(x, x, g.reshape(1, d), w, scale.reshape(1, d))


def _rope_full(x, cos, sin_signed):
    return x * cos + pltpu.roll(x, HEAD_DIM // 2, 1) * sin_signed


def _rope_idx(x, cos, sin_signed, first_half):
    q = IDX_DIM // 2
    rot = jnp.where(first_half, pltpu.roll(x, LANES - q, 1), pltpu.roll(x, q, 1))
    return x * cos + rot * sin_signed


def _proj_kernel(x_ref, g_ref, w_ref, wt_ref, c128_ref, s128_ref, c64_ref, s64_ref,
                 p_ref, ike_ref, iko_ref, iw_ref, h_ref, *, n_q_blocks, n_rope_blocks, n_main, q_scale, iw_scale):
    j = pl.program_id(1)
    tm = x_ref.shape[0]
    cb = w_ref.shape[1]
    lane = lax.broadcasted_iota(I32, (tm, LANES), 1)
    first_half = (lane % IDX_DIM) < (IDX_DIM // 2)

    @pl.when(j == 0)
    def _():
        h_ref[...] = _rms(x_ref[...], g_ref[...]).astype(BF16)

    @pl.when(j < n_rope_blocks)
    def _():
        r = _dot(h_ref[...], w_ref[...]) * jnp.where(j < n_q_blocks, q_scale, 1.0)
        cos, sin = c128_ref[...], s128_ref[...]
        for c in range(cb // LANES):
            cs = slice(c * LANES, (c + 1) * LANES)
            p_ref[:, cs] = _rope_full(r[:, cs], cos, sin).astype(BF16)

    @pl.when(j == n_rope_blocks)
    def _():
        p_ref[...] = _dot(h_ref[...], w_ref[...]).astype(BF16)

    @pl.when((j > n_rope_blocks) & (j < n_main))
    def _():
        r = _dot(h_ref[...], w_ref[...])
        cos, sin = c64_ref[...], s64_ref[...]
        for c in range(cb // LANES):
            cs = slice(c * LANES, (c + 1) * LANES)
            p_ref[:, cs] = _rope_idx(r[:, cs], cos, sin, first_half).astype(BF16)

    @pl.when(j == n_main)
    def _():
        r = _dot(h_ref[...], wt_ref[...])
        roped = _rope_idx(r, c64_ref[...], s64_ref[...], first_half)
        ike = jnp.where(lane < IDX_DIM, roped, 0.0)
        ike_ref[...] = ike.astype(BF16)
        iko_ref[...] = pltpu.roll(ike, IDX_DIM, 1).astype(BF16)
        iw_ref[...] = r * iw_scale


def _rope_tables(seq_len):
    pos = jnp.arange(seq_len, dtype=F32)[:, None]

    def tables(dim):
        inv = ROPE_THETA ** (-jnp.arange(0, dim, 2, dtype=F32) / dim)
        ang = pos * inv[None, :]
        cos, sin = jnp.cos(ang), jnp.sin(ang)
        reps = LANES // dim
        return (jnp.tile(jnp.concatenate([cos, cos], axis=1), (1, reps)),
                jnp.tile(jnp.concatenate([-sin, sin], axis=1), (1, reps)))

    return tables(HEAD_DIM) + tables(IDX_DIM)


def _dsa_project(x, g, w_in, seq_len):
    m, d = x.shape
    q_cols = d
    kv_cols = N_KV_HEADS * HEAD_DIM
    iq_cols = IDX_HEADS * IDX_DIM
    main_cols = q_cols + 2 * kv_cols + iq_cols
    cb = min(COL_TILE, kv_cols)
    assert kv_cols % cb == 0 and q_cols % cb == 0 and iq_cols % cb == 0 and kv_cols == cb
    n_rope_blocks = (q_cols + kv_cols) // cb
    n_main = main_cols // cb
    assert IDX_DIM + IDX_HEADS <= LANES
    w_main = w_in[:, :main_cols].astype(BF16)
    w_tail = jnp.pad(w_in[:, main_cols:], ((0, 0), (0, LANES - IDX_DIM - IDX_HEADS))).astype(BF16)
    c128, s128, c64, s64 = _rope_tables(seq_len)
    tm = min(MLP_ROW_TILE, seq_len)
    tps = seq_len // tm
    tab = pl.BlockSpec((tm, LANES), lambda i, j: (i % tps, 0))
    tail = pl.BlockSpec((tm, LANES), lambda i, j: (i, 0))
    return pl.pallas_call(
        functools.partial(_proj_kernel, n_q_blocks=q_cols // cb, n_rope_blocks=n_rope_blocks, n_main=n_main,
                          q_scale=(HEAD_DIM ** -0.5) * LOG2_E, iw_scale=(IDX_HEADS ** -0.5) * (IDX_DIM ** -0.5)),
        grid=(m // tm, n_main + 1),
        in_specs=[
            pl.BlockSpec((tm, d), lambda i, j: (i, 0)),
            pl.BlockSpec((1, d), lambda i, j: (0, 0)),
            pl.BlockSpec((d, cb), lambda i, j: (0, jnp.minimum(j, n_main - 1))),
            pl.BlockSpec((d, LANES), lambda i, j: (0, 0)),
            tab, tab, tab, tab,
        ],
        out_specs=[
            pl.BlockSpec((tm, cb), lambda i, j: (i, jnp.minimum(j, n_main - 1))),
            tail, tail, tail,
        ],
        out_shape=[
            jax.ShapeDtypeStruct((m, main_cols), BF16),
            jax.ShapeDtypeStruct((m, LANES), BF16),
            jax.ShapeDtypeStruct((m, LANES), BF16),
            jax.ShapeDtypeStruct((m, LANES), F32),
        ],
        scratch_shapes=[pltpu.VMEM((tm, d), BF16)],
        compiler_params=_params("parallel", "arbitrary"),
        name="dsa_project",
    )(x, g.reshape(1, d), w_main, w_tail, c128, s128, c64, s64)


def _dsa_kernel(q_ref, iq_ref, k_ref, v_ref, ike_ref, iko_ref, iw_ref, o_ref,
                keyt_ref, key16_ref, qst_ref, iwt_ref, m_ref, acc_ref, *, top_k, n_heads, kb):
    i = pl.program_id(1)
    t = q_ref.shape[1]
    seq_len = k_ref.shape[1]
    rep = n_heads // N_KV_HEADS
    n_blocks = (i * t) // kb + 1
    idx_bits = (seq_len - 1).bit_length()

    for h in range(n_heads):
        qst_ref[h * t:(h + 1) * t, :] = q_ref[0, :, h * HEAD_DIM:(h + 1) * HEAD_DIM]
    iwt_ref[...] = iw_ref[0].T

    krow_local = lax.broadcasted_iota(I32, (kb, t), 0)
    qpos = i * t + lax.broadcasted_iota(I32, (1, t), 1)

    def score_block(b, carry):
        s0 = pl.multiple_of(b * kb, kb)
        ike = ike_ref[0, pl.ds(s0, kb), :]
        iko = iko_ref[0, pl.ds(s0, kb), :]
        sc = jnp.zeros((kb, t), F32)
        for p in range(IDX_HEADS // 2):
            iqp = iq_ref[0, :, p * LANES:(p + 1) * LANES]
            r0 = IDX_DIM + 2 * p
            sc = sc + iwt_ref[r0:r0 + 1, :] * jnp.maximum(_dot_nt(ike, iqp), 0.0)
            sc = sc + iwt_ref[r0 + 1:r0 + 2, :] * jnp.maximum(_dot_nt(iko, iqp), 0.0)
        bits = lax.bitcast_convert_type(sc, I32)
        key = jnp.where(bits < 0, bits ^ 0x7FFFFFFF, bits)
        key = jnp.where(s0 + krow_local <= qpos, key, INT_MIN)
        keyt_ref[pl.ds(s0, kb), :] = key
        key16_ref[pl.ds(s0, kb), :] = jnp.right_shift(key, 16).astype(I16)
        return carry

    lax.fori_loop(0, n_blocks, score_block, 0)

    def count_where(pred):
        def body(b, cnt):
            s0 = pl.multiple_of(b * kb, kb)
            hit = pred(keyt_ref[pl.ds(s0, kb), :], s0 + krow_local).astype(I32)
            part = jnp.sum(hit.reshape(COUNT_CHAINS, kb // (COUNT_CHAINS * SUBLANES), SUBLANES, t), axis=1)
            return cnt + jnp.sum(part, axis=0)

        cnt = lax.fori_loop(0, n_blocks, body, jnp.zeros((SUBLANES, t), I32))
        return jnp.sum(cnt, axis=0, keepdims=True)

    def count_ge(cand):
        return count_where(lambda kk, krow: kk >= cand)

    rows16 = 2 * SUBLANES

    def count16(pred):
        def body(b, cnt):
            s0 = pl.multiple_of(b * kb, kb)
            hit = pred(key16_ref[pl.ds(s0, kb), :]).astype(I16)
            tiles = [hit[r:r + rows16] for r in range(0, kb, rows16)]
            chains = [functools.reduce(lax.add, tiles[c::COUNT_CHAINS]) for c in range(COUNT_CHAINS)]
            return cnt + functools.reduce(lax.add, chains)

        cnt = lax.fori_loop(0, n_blocks, body, jnp.zeros((rows16, t), I16))
        return jnp.sum(cnt.astype(I32), axis=0, keepdims=True)

    def count_ge_hi(cand):
        cand_hi = jnp.right_shift(cand, 16).astype(I16)
        return count16(lambda k16: k16 >= cand_hi)

    c0 = count_ge_hi(jnp.zeros((1, t), I32))
    tau = jnp.where(c0 >= top_k, 0, INT_MIN).astype(I32)
    n_sel = jnp.where(c0 >= top_k, c0, 0)

    def n_pending(n_sel):
        return jnp.max(jnp.where((qpos >= top_k) & (n_sel != top_k), 1, 0))

    def bit_search(counter, first, last, carry):
        def cond(c):
            return (c[0] < last) & (c[3] > 0)

        def body(c):
            bi, tau, n_sel, _ = c
            for _ in range(BIT_STEPS_PER_CHECK):
                cand = tau | jnp.left_shift(jnp.int32(1), jnp.maximum(30 - bi, 0))
                n = counter(cand)
                keep = (n >= top_k) & (bi < last)
                tau = jnp.where(keep, cand, tau)
                n_sel = jnp.where(keep, n, n_sel)
                bi = bi + 1
            return bi, tau, n_sel, n_pending(n_sel)

        _, tau, n_sel, pending = lax.while_loop(cond, body, (jnp.int32(first),) + carry)
        return tau, n_sel, pending

    carry = bit_search(count_ge_hi, 0, HI_BIT_STEPS, (tau, n_sel, n_pending(n_sel)))

    tau_hi = jnp.right_shift(carry[0], 16)
    tau_hi16 = tau_hi.astype(I16)
    n_above = count16(lambda k16: k16 > tau_hi16)

    @pl.when(carry[2] > 0)
    def _():
        def build_low(b, c):
            s0 = pl.multiple_of(b * kb, kb)
            kk = keyt_ref[pl.ds(s0, kb), :]
            low = (kk & 0xFFFF) - 32768
            key16_ref[pl.ds(s0, kb), :] = jnp.where(jnp.right_shift(kk, 16) == tau_hi, low, -32768).astype(I16)
            return c

        lax.fori_loop(0, n_blocks, build_low, 0)

    def count_ge_lo(cand):
        cand_lo = ((cand & 0xFFFF) - 32768).astype(I16)
        return n_above + count16(lambda k16: k16 >= cand_lo)

    tau, n_sel, _ = bit_search(count_ge_lo, HI_BIT_STEPS, 31, carry)

    @pl.when(jnp.max(n_sel) > top_k)
    def _():
        need = top_k - count_ge(tau + 1)

        def idx_body(bi, cut):
            cand = cut | jnp.left_shift(jnp.int32(1), idx_bits - 1 - bi)
            n = count_where(lambda kk, krow: (kk == tau) & (krow < cand))
            return jnp.where(n < need, cand, cut)

        cut = lax.fori_loop(0, idx_bits, idx_body, jnp.zeros((1, t), I32))
        cut = jnp.where(n_sel > top_k, cut, seq_len)

        def demote(b, carry):
            s0 = pl.multiple_of(b * kb, kb)
            kk = keyt_ref[pl.ds(s0, kb), :]
            keyt_ref[pl.ds(s0, kb), :] = jnp.where((kk == tau) & (s0 + krow_local > cut), tau - 1, kk)
            return carry

        lax.fori_loop(0, n_blocks, demote, 0)

    tau = jnp.maximum(tau, INT_MIN + 1)

    m_ref[...] = jnp.full(m_ref.shape, MASKED_SCORE, F32)
    acc_ref[...] = jnp.zeros(acc_ref.shape, F32)
    ones = jnp.ones((kb, HEAD_DIM), BF16)

    def att_block(b, carry):
        s0 = pl.multiple_of(b * kb, kb)
        neg = jnp.where(keyt_ref[pl.ds(s0, kb), :] >= tau, 0.0, MASKED_SCORE).T
        for g in range(N_KV_HEADS):
            gs = slice(g * HEAD_DIM, (g + 1) * HEAD_DIM)
            grows = slice(g * rep * t, (g + 1) * rep * t)
            s = _dot_nt(qst_ref[grows, :], k_ref[0, pl.ds(s0, kb), gs])
            v1 = jnp.concatenate([v_ref[0, pl.ds(s0, kb), gs], ones], axis=1)
            ps = []
            alphas = []
            for r in range(rep):
                hrows = slice((g * rep + r) * t, (g * rep + r + 1) * t)
                sr = s[r * t:(r + 1) * t] + neg
                tiles = [sr[:, c * LANES:(c + 1) * LANES] for c in range(kb // LANES)]
                mx = functools.reduce(jnp.maximum, tiles)
                m_old = m_ref[hrows, :]
                m_new = jnp.maximum(m_old, jnp.max(mx, axis=1, keepdims=True))
                m_ref[hrows, :] = m_new
                alphas.append(jnp.exp2(m_old - m_new))
                ps.append(jnp.concatenate([jnp.exp2(x - m_new) for x in tiles], axis=1).astype(BF16))
            pv = _dot(jnp.concatenate(ps, axis=0), v1)
            for r in range(rep):
                hrows = slice((g * rep + r) * t, (g * rep + r + 1) * t)
                for c in range(2):
                    cs = slice(c * HEAD_DIM, (c + 1) * HEAD_DIM)
                    acc_ref[hrows, cs] = alphas[r] * acc_ref[hrows, cs] + pv[r * t:(r + 1) * t, cs]
        return carry

    lax.fori_loop(0, n_blocks, att_block, 0)

    for h in range(n_heads):
        hrows = slice(h * t, (h + 1) * t)
        o = acc_ref[hrows, 0:HEAD_DIM] / acc_ref[hrows, HEAD_DIM:2 * HEAD_DIM]
        o_ref[0, :, h * HEAD_DIM:(h + 1) * HEAD_DIM] = o.astype(BF16)


def _dsa_attend(p, ike, iko, iw, batch, seq_len, d):
    n_heads = d // HEAD_DIM
    kv_cols = N_KV_HEADS * HEAD_DIM
    iq_cols = IDX_HEADS * IDX_DIM
    t = min(DSA_TILE, seq_len)
    kb = min(DSA_KEY_BLOCK, seq_len)
    top_k = min(TOP_K_MAX, seq_len // 4)
    assert (d + 2 * kv_cols) % iq_cols == 0 and d % kv_cols == 0 and t % LANES == 0 and kb % t == 0
    p3 = p.reshape(batch, seq_len, p.shape[-1])
    whole = functools.partial(pl.BlockSpec, pipeline_mode=pl.Buffered(1))
    out = pl.pallas_call(
        functools.partial(_dsa_kernel, top_k=top_k, n_heads=n_heads, kb=kb),
        grid=(batch, seq_len // t),
        in_specs=[
            pl.BlockSpec((1, t, d), lambda b, i: (b, i, 0)),
            pl.BlockSpec((1, t, iq_cols), lambda b, i: (b, i, (d + 2 * kv_cols) // iq_cols)),
            whole((1, seq_len, kv_cols), lambda b, i: (b, 0, d // kv_cols)),
            whole((1, seq_len, kv_cols), lambda b, i: (b, 0, d // kv_cols + 1)),
            whole((1, seq_len, LANES), lambda b, i: (b, 0, 0)),
            whole((1, seq_len, LANES), lambda b, i: (b, 0, 0)),
            pl.BlockSpec((1, t, LANES), lambda b, i: (b, i, 0)),
        ],
        out_specs=pl.BlockSpec((1, t, d), lambda b, i: (b, i, 0)),
        out_shape=jax.ShapeDtypeStruct((batch, seq_len, d), BF16),
        scratch_shapes=[
            pltpu.VMEM((seq_len, t), I32),
            pltpu.VMEM((seq_len, t), I16),
            pltpu.VMEM((n_heads * t, HEAD_DIM), BF16),
            pltpu.VMEM((LANES, t), F32),
            pltpu.VMEM((n_heads * t, LANES), F32),
            pltpu.VMEM((n_heads * t, 2 * HEAD_DIM), F32),
        ],
        compiler_params=_params("parallel", "arbitrary"),
        name="dsa_attend",
    )(p3, p3, p3, p3,
      ike.reshape(batch, seq_len, LANES), iko.reshape(batch, seq_len, LANES),
      iw.reshape(batch, seq_len, LANES))
    return out.reshape(batch * seq_len, d)


def _proj_res_kernel(a_ref, w_ref, b_ref, x_ref, o_ref):
    o_ref[...] = x_ref[...] + _dot(a_ref[...], w_ref[...]) + b_ref[...]


def _proj_residual(a, w, b, x):
    m, d = x.shape
    kdim = a.shape[1]
    tm = min(ROW_TILE, m)
    return pl.pallas_call(
        _proj_res_kernel,
        grid=(m // tm,),
        in_specs=[
            pl.BlockSpec((tm, kdim), lambda i: (i, 0)),
            pl.BlockSpec((kdim, d), lambda i: (0, 0)),
            pl.BlockSpec((1, d), lambda i: (0, 0)),
            pl.BlockSpec((tm, d), lambda i: (i, 0)),
        ],
        out_specs=pl.BlockSpec((tm, d), lambda i: (i, 0)),
        out_shape=jax.ShapeDtypeStruct((m, d), F32),
        compiler_params=_params("parallel"),
        name="proj_residual",
    )(a, w, b.reshape(1, d), x)


def _glu_kernel(x_ref, g_ref, wa_ref, wg_ref, ba_ref, bg_ref, o_ref, h_ref):
    @pl.when(pl.program_id(1) == 0)
    def _():
        h_ref[...] = _rms(x_ref[...], g_ref[...]).astype(BF16)

    h = h_ref[...]
    a = _dot(h, wa_ref[...]) + ba_ref[...]
    gate = _dot(h, wg_ref[...]) + bg_ref[...]
    o_ref[...] = a * jax.nn.sigmoid(gate)


def _glu(x, g, w, b):
    m, d = x.shape
    tm, tn = min(MLP_ROW_TILE, m), min(COL_TILE, d)
    nb = d // tn
    b2 = b.reshape(1, 2 * d)
    return pl.pallas_call(
        _glu_kernel,
        grid=(m // tm, nb),
        in_specs=[
            pl.BlockSpec((tm, d), lambda i, j: (i, 0)),
            pl.BlockSpec((1, d), lambda i, j: (0, 0)),
            pl.BlockSpec((d, tn), lambda i, j: (0, j)),
            pl.BlockSpec((d, tn), lambda i, j: (0, j + nb)),
            pl.BlockSpec((1, tn), lambda i, j: (0, j)),
            pl.BlockSpec((1, tn), lambda i, j: (0, j + nb)),
        ],
        out_specs=pl.BlockSpec((tm, tn), lambda i, j: (i, j)),
        out_shape=jax.ShapeDtypeStruct((m, d), F32),
        scratch_shapes=[pltpu.VMEM((tm, d), BF16)],
        compiler_params=_params("parallel", "arbitrary"),
        name="conv_glu",
    )(x, g.reshape(1, d), w, w, b2, b2)


def _dwconv_kernel(u_ref, uh_ref, w_ref, b_ref, lg_ref, lb_ref, o_ref, sh_ref, c_ref, *, tiles_per_seq):
    i = pl.program_id(0)
    tm, d = u_ref.shape
    rows = tm + CONV_HALO
    halo = jnp.where(i % tiles_per_seq == 0, 0.0, uh_ref[...])
    off = CONV_HALO - (CONV_WIDTH - 1)
    cw = sh_ref.shape[2]
    rc = min(CONV_ROW_CHUNK, tm)
    for c0 in range(0, d, cw):
        cols = slice(c0, c0 + cw)
        ext = jnp.concatenate([halo[:, cols], u_ref[:, cols]], axis=0)
        sh_ref[0] = ext
        for j in range(1, SUBLANES):
            sh_ref[j] = pltpu.roll(ext, rows - j, 0)
        for r0 in range(0, tm, rc):
            acc = jnp.zeros((rc, cw), F32)
            for k in range(CONV_WIDTH):
                j = (off + k) % SUBLANES
                a = off + k - j + r0
                acc = acc + sh_ref[j, a:a + rc, :] * w_ref[k:k + 1, cols]
            c_ref[r0:r0 + rc, cols] = acc + b_ref[:, cols]
    c = c_ref[...]
    mu = jnp.mean(c, axis=-1, keepdims=True)
    cc = c - mu
    var = jnp.mean(cc * cc, axis=-1, keepdims=True)
    y = cc * lax.rsqrt(var + EPS) * lg_ref[...] + lb_ref[...]
    o_ref[...] = (y * jax.nn.sigmoid(y)).astype(BF16)


def _dwconv(u, w_dw, b_dw, ln_g, ln_b, seq_len):
    m, d = u.shape
    tm = min(CONV_ROW_TILE, seq_len)
    halo_blocks = tm // CONV_HALO
    row = lambda v: v.reshape(1, d)
    return pl.pallas_call(
        functools.partial(_dwconv_kernel, tiles_per_seq=seq_len // tm),
        grid=(m // tm,),
        in_specs=[
            pl.BlockSpec((tm, d), lambda i: (i, 0)),
            pl.BlockSpec((CONV_HALO, d), lambda i: (jnp.maximum(i * halo_blocks - 1, 0), 0)),
            pl.BlockSpec((CONV_WIDTH, d), lambda i: (0, 0)),
            pl.BlockSpec((1, d), lambda i: (0, 0)),
            pl.BlockSpec((1, d), lambda i: (0, 0)),
            pl.BlockSpec((1, d), lambda i: (0, 0)),
        ],
        out_specs=pl.BlockSpec((tm, d), lambda i: (i, 0)),
        out_shape=jax.ShapeDtypeStruct((m, d), BF16),
        scratch_shapes=[pltpu.VMEM((SUBLANES, tm + CONV_HALO, min(COL_TILE, d)), F32), pltpu.VMEM((tm, d), F32)],
        compiler_params=_params("parallel"),
        name="conv_dw_ln",
    )(u, u, w_dw, row(b_dw), row(ln_g), row(ln_b))


def kernel(x, norm_mix, norm_mlp, mlp_up, mlp_down, pool_w, pool_scale, dsa_w_in, dsa_w_out,
           conv_w_pw1, conv_b_pw1, conv_w_dw, conv_b_dw, conv_ln_g, conv_ln_b, conv_w_pw2,
           conv_b_pw2, norm_final):
    batch, seq_len, d = x.shape
    depth = norm_mix.shape[0]
    xf = x.reshape(batch * seq_len, d)
    for i in range(depth):
        kind, j = i % N_MIXERS, i // N_MIXERS
        if kind == 0:
            xf = _pool_layer(xf, norm_mix[i], pool_w[j].astype(BF16), pool_scale[j], seq_len)
        elif kind == 1:
            p, ike, iko, iw = _dsa_project(xf, norm_mix[i], dsa_w_in[j], seq_len)
            o = _dsa_attend(p, ike, iko, iw, batch, seq_len, d)
            xf = _proj_residual(o, dsa_w_out[j].astype(BF16), jnp.zeros((d,), F32), xf)
        else:
            u = _glu(xf, norm_mix[i], conv_w_pw1[j].astype(BF16), conv_b_pw1[j])
            u = _dwconv(u, conv_w_dw[j], conv_b_dw[j], conv_ln_g[j], conv_ln_b[j], seq_len)
            xf = _proj_residual(u, conv_w_pw2[j].astype(BF16), conv_b_pw2[j], xf)
        xf = _mlp(xf, norm_mlp[i], mlp_up[i].astype(BF16), mlp_down[i].astype(BF16),
                  norm_final, final_norm=(i == depth - 1))
    return xf.reshape(batch, seq_len, d)
```

```python
import functools

import jax
import jax.numpy as jnp
from jax import lax
from jax.experimental import pallas as pl
from jax.experimental.pallas import tpu as pltpu

F32 = jnp.float32
BF16 = jnp.bfloat16
I32 = jnp.int32
I16 = jnp.int16

N_MIXERS = 3
POOL_WINDOWS = (2, 4, 8, 16)
HEAD_DIM = 128
N_KV_HEADS = 4
IDX_HEADS = 16
IDX_DIM = 64
TOP_K_MAX = 256
ROPE_THETA = 10000.0
CONV_WIDTH = 31
EPS = 1e-6

LANES = 128
SUBLANES = 8
VMEM_LIMIT_BYTES = 56 * 1024 * 1024

ROW_TILE = 512
MLP_ROW_TILE = 1024
FF_TILE = 512
COL_TILE = 512
DSA_TILE = 256
DSA_KEY_BLOCK = 512
COUNT_CHAINS = 8
BIT_STEPS_PER_CHECK = 4
HI_BIT_STEPS = 15
PROJ_ROW_CHUNK = 256
CONV_ROW_TILE = 128
CONV_ROW_CHUNK = 32
POOL_HALO = 16
CONV_HALO = 32

INT_MIN = -(2 ** 31)
MASKED_SCORE = -1e30
LOG2_E = 1.4426950408889634


def _params(*sem):
    return pltpu.CompilerParams(dimension_semantics=sem, vmem_limit_bytes=VMEM_LIMIT_BYTES)


def _rms(x, g):
    ms = jnp.mean(x * x, axis=-1, keepdims=True)
    return x * lax.rsqrt(ms + EPS) * g


def _dot(a, b):
    return jnp.dot(a, b, preferred_element_type=F32)


def _dot_nt(a, b):
    return lax.dot_general(a, b, (((1,), (1,)), ((), ())), preferred_element_type=F32)


def _mlp_kernel(x_ref, g_ref, up_ref, down_ref, fg_ref, o_ref, h_ref, *, final_norm):
    j = pl.program_id(1)

    @pl.when(j == 0)
    def _():
        x = x_ref[...]
        h_ref[...] = _rms(x, g_ref[...]).astype(BF16)
        o_ref[...] = x

    a = jnp.maximum(_dot(h_ref[...], up_ref[...]), 0.0)
    o_ref[...] += _dot((a * a).astype(BF16), down_ref[...])

    if final_norm:
        @pl.when(j == pl.num_programs(1) - 1)
        def _():
            o_ref[...] = _rms(o_ref[...], fg_ref[...])


def _mlp(x, g, up, down, final_g, final_norm):
    m, d = x.shape
    f = up.shape[1]
    tm, tf = min(MLP_ROW_TILE, m), min(FF_TILE, f)
    return pl.pallas_call(
        functools.partial(_mlp_kernel, final_norm=final_norm),
        grid=(m // tm, f // tf),
        in_specs=[
            pl.BlockSpec((tm, d), lambda i, j: (i, 0)),
            pl.BlockSpec((1, d), lambda i, j: (0, 0)),
            pl.BlockSpec((d, tf), lambda i, j: (0, j)),
            pl.BlockSpec((tf, d), lambda i, j: (j, 0)),
            pl.BlockSpec((1, d), lambda i, j: (0, 0)),
        ],
        out_specs=pl.BlockSpec((tm, d), lambda i, j: (i, 0)),
        out_shape=jax.ShapeDtypeStruct((m, d), F32),
        scratch_shapes=[pltpu.VMEM((tm, d), BF16)],
        compiler_params=_params("parallel", "arbitrary"),
        name="mlp",
    )(x, g.reshape(1, d), up, down, final_g.reshape(1, d))


def _pool_kernel(x_ref, xh_ref, g_ref, w_ref, sc_ref, o_ref, ext_ref, *, tiles_per_seq):
    i = pl.program_id(0)
    tm, d = x_ref.shape
    c = d // len(POOL_WINDOWS)
    seq_tile = i % tiles_per_seq
    x = x_ref[...]
    g = g_ref[...]
    h = _rms(x, g)
    ext_ref[0:POOL_HALO, :] = jnp.where(seq_tile == 0, 0.0, _rms(xh_ref[...], g))
    ext_ref[POOL_HALO:, :] = h
    pos = (seq_tile * tm + lax.broadcasted_iota(I32, (tm, 1), 0)).astype(F32)
    for gi, w in enumerate(POOL_WINDOWS):
        cols = slice(gi * c, (gi + 1) * c)
        acc = ext_ref[POOL_HALO:POOL_HALO + tm, cols]
        for j in range(1, w):
            acc = acc + ext_ref[POOL_HALO - j:POOL_HALO - j + tm, cols]
        y = acc / jnp.minimum(pos + 1.0, float(w)) - h[:, cols]
        z = _dot(y.astype(BF16), w_ref[gi])
        o_ref[:, cols] = x[:, cols] + z * sc_ref[:, cols]


def _pool_layer(x, g, w, scale, seq_len):
    m, d = x.shape
    tm = min(ROW_TILE, seq_len)
    ng, c, _ = w.shape
    halo_blocks = tm // POOL_HALO
    return pl.pallas_call(
        functools.partial(_pool_kernel, tiles_per_seq=seq_len // tm),
        grid=(m // tm,),
        in_specs=[
            pl.BlockSpec((tm, d), lambda i: (i, 0)),
            pl.BlockSpec((POOL_HALO, d), lambda i: (jnp.maximum(i * halo_blocks - 1, 0), 0)),
            pl.BlockSpec((1, d), lambda i: (0, 0)),
            pl.BlockSpec((ng, c, c), lambda i: (0, 0, 0)),
            pl.BlockSpec((1, d), lambda i: (0, 0)),
        ],
        out_specs=pl.BlockSpec((tm, d), lambda i: (i, 0)),
        out_shape=jax.ShapeDtypeStruct((m, d), F32),
        scratch_shapes=[pltpu.VMEM((tm + POOL_HALO, d), F32)],
        compiler_params=_params("parallel"),
        name="pool_mixer",
    )(x, x, g.reshape(1, d), w, scale.reshape(1, d))


def _rope_full(x, cos, sin_signed):
    return x * cos + pltpu.roll(x, HEAD_DIM // 2, 1) * sin_signed


def _rope_idx(x, cos, sin_signed, first_half):
    q = IDX_DIM // 2
    rot = jnp.where(first_half, pltpu.roll(x, LANES - q, 1), pltpu.roll(x, q, 1))
    return x * cos + rot * sin_signed


def _proj_kernel(x_ref, g_ref, w_ref, wt_ref, c128_ref, s128_ref, c64_ref, s64_ref,
                 p_ref, ike_ref, iko_ref, iw_ref, h_ref, *, n_q_blocks, n_rope_blocks, n_main, q_scale, iw_scale):
    j = pl.program_id(1)
    tm = x_ref.shape[0]
    cb = w_ref.shape[1]
    lane = lax.broadcasted_iota(I32, (tm, LANES), 1)
    first_half = (lane % IDX_DIM) < (IDX_DIM // 2)
    rc = min(PROJ_ROW_CHUNK, tm)
    row_chunks = [slice(r0, r0 + rc) for r0 in range(0, tm, rc)]
    first_half_chunk = (lax.broadcasted_iota(I32, (rc, LANES), 1) % IDX_DIM) < (IDX_DIM // 2)

    @pl.when(j == 0)
    def _():
        h_ref[...] = _rms(x_ref[...], g_ref[...]).astype(BF16)

    @pl.when(j < n_rope_blocks)
    def _():
        factor = jnp.where(j < n_q_blocks, q_scale, 1.0)
        for rs in row_chunks:
            r = _dot(h_ref[rs, :], w_ref[...]) * factor
            cos, sin = c128_ref[rs, :], s128_ref[rs, :]
            for c in range(cb // LANES):
                cs = slice(c * LANES, (c + 1) * LANES)
                p_ref[rs, cs] = _rope_full(r[:, cs], cos, sin).astype(BF16)

    @pl.when(j == n_rope_blocks)
    def _():
        p_ref[...] = _dot(h_ref[...], w_ref[...]).astype(BF16)

    @pl.when((j > n_rope_blocks) & (j < n_main))
    def _():
        for rs in row_chunks:
            r = _dot(h_ref[rs, :], w_ref[...])
            cos, sin = c64_ref[rs, :], s64_ref[rs, :]
            for c in range(cb // LANES):
                cs = slice(c * LANES, (c + 1) * LANES)
                p_ref[rs, cs] = _rope_idx(r[:, cs], cos, sin, first_half_chunk).astype(BF16)

    @pl.when(j == n_main)
    def _():
        r = _dot(h_ref[...], wt_ref[...])
        roped = _rope_idx(r, c64_ref[...], s64_ref[...], first_half)
        ike = jnp.where(lane < IDX_DIM, roped, 0.0)
        ike_ref[...] = ike.astype(BF16)
        iko_ref[...] = pltpu.roll(ike, IDX_DIM, 1).astype(BF16)
        iw_ref[...] = r * iw_scale


def _rope_tables(seq_len):
    pos = jnp.arange(seq_len, dtype=F32)[:, None]

    def tables(dim):
        inv = ROPE_THETA ** (-jnp.arange(0, dim, 2, dtype=F32) / dim)
        ang = pos * inv[None, :]
        cos, sin = jnp.cos(ang), jnp.sin(ang)
        reps = LANES // dim
        return (jnp.tile(jnp.concatenate([cos, cos], axis=1), (1, reps)),
                jnp.tile(jnp.concatenate([-sin, sin], axis=1), (1, reps)))

    return tables(HEAD_DIM) + tables(IDX_DIM)


def _dsa_project(x, g, w_in, seq_len):
    m, d = x.shape
    q_cols = d
    kv_cols = N_KV_HEADS * HEAD_DIM
    iq_cols = IDX_HEADS * IDX_DIM
    main_cols = q_cols + 2 * kv_cols + iq_cols
    cb = min(COL_TILE, kv_cols)
    assert kv_cols % cb == 0 and q_cols % cb == 0 and iq_cols % cb == 0 and kv_cols == cb
    n_rope_blocks = (q_cols + kv_cols) // cb
    n_main = main_cols // cb
    assert IDX_DIM + IDX_HEADS <= LANES
    w_main = w_in[:, :main_cols].astype(BF16)
    w_tail = jnp.pad(w_in[:, main_cols:], ((0, 0), (0, LANES - IDX_DIM - IDX_HEADS))).astype(BF16)
    c128, s128, c64, s64 = _rope_tables(seq_len)
    tm = min(MLP_ROW_TILE, seq_len)
    tps = seq_len // tm
    tab = pl.BlockSpec((tm, LANES), lambda i, j: (i % tps, 0))
    tail = pl.BlockSpec((tm, LANES), lambda i, j: (i, 0))
    return pl.pallas_call(
        functools.partial(_proj_kernel, n_q_blocks=q_cols // cb, n_rope_blocks=n_rope_blocks, n_main=n_main,
                          q_scale=(HEAD_DIM ** -0.5) * LOG2_E, iw_scale=(IDX_HEADS ** -0.5) * (IDX_DIM ** -0.5)),
        grid=(m // tm, n_main + 1),
        in_specs=[
            pl.BlockSpec((tm, d), lambda i, j: (i, 0)),
            pl.BlockSpec((1, d), lambda i, j: (0, 0)),
            pl.BlockSpec((d, cb), lambda i, j: (0, jnp.minimum(j, n_main - 1))),
            pl.BlockSpec((d, LANES), lambda i, j: (0, 0)),
            tab, tab, tab, tab,
        ],
        out_specs=[
            pl.BlockSpec((tm, cb), lambda i, j: (i, jnp.minimum(j, n_main - 1))),
            tail, tail, tail,
        ],
        out_shape=[
            jax.ShapeDtypeStruct((m, main_cols), BF16),
            jax.ShapeDtypeStruct((m, LANES), BF16),
            jax.ShapeDtypeStruct((m, LANES), BF16),
            jax.ShapeDtypeStruct((m, LANES), F32),
        ],
        scratch_shapes=[pltpu.VMEM((tm, d), BF16)],
        compiler_params=_params("parallel", "arbitrary"),
        name="dsa_project",
    )(x, g.reshape(1, d), w_main, w_tail, c128, s128, c64, s64)


def _dsa_kernel(q_ref, iq_ref, k_ref, v_ref, ike_ref, iko_ref, iw_ref, o_ref,
                keyt_ref, key16_ref, qst_ref, iwt_ref, m_ref, acc_ref, *, top_k, n_heads, kb):
    i = pl.program_id(1)
    t = q_ref.shape[1]
    seq_len = k_ref.shape[1]
    rep = n_heads // N_KV_HEADS
    n_blocks = (i * t) // kb + 1
    idx_bits = (seq_len - 1).bit_length()

    for h in range(n_heads):
        qst_ref[h * t:(h + 1) * t, :] = q_ref[0, :, h * HEAD_DIM:(h + 1) * HEAD_DIM]
    iwt_ref[...] = iw_ref[0].T

    krow_local = lax.broadcasted_iota(I32, (kb, t), 0)
    qpos = i * t + lax.broadcasted_iota(I32, (1, t), 1)

    def score_block(b, carry):
        s0 = pl.multiple_of(b * kb, kb)
        ike = ike_ref[0, pl.ds(s0, kb), :]
        iko = iko_ref[0, pl.ds(s0, kb), :]
        dots = []
        for p in range(IDX_HEADS // 2):
            iqp = iq_ref[0, :, p * LANES:(p + 1) * LANES]
            dots += [_dot_nt(ike, iqp), _dot_nt(iko, iqp)]
        sc = jnp.zeros((kb, t), F32)
        for h, d in enumerate(dots):
            sc = sc + iwt_ref[IDX_DIM + h:IDX_DIM + h + 1, :] * jnp.maximum(d, 0.0)
        bits = lax.bitcast_convert_type(sc, I32)
        key = jnp.where(bits < 0, bits ^ 0x7FFFFFFF, bits)
        key = jnp.where(s0 + krow_local <= qpos, key, INT_MIN)
        keyt_ref[pl.ds(s0, kb), :] = key
        key16_ref[pl.ds(s0, kb), :] = jnp.right_shift(key, 16).astype(I16)
        return carry

    lax.fori_loop(0, n_blocks, score_block, 0)

    def count_where(pred):
        def body(b, cnt):
            s0 = pl.multiple_of(b * kb, kb)
            hit = pred(keyt_ref[pl.ds(s0, kb), :], s0 + krow_local).astype(I32)
            part = jnp.sum(hit.reshape(COUNT_CHAINS, kb // (COUNT_CHAINS * SUBLANES), SUBLANES, t), axis=1)
            return cnt + jnp.sum(part, axis=0)

        cnt = lax.fori_loop(0, n_blocks, body, jnp.zeros((SUBLANES, t), I32))
        return jnp.sum(cnt, axis=0, keepdims=True)

    def count_ge(cand):
        return count_where(lambda kk, krow: kk >= cand)

    rows16 = 2 * SUBLANES

    def count16(pred):
        def body(b, cnt):
            s0 = pl.multiple_of(b * kb, kb)
            hit = pred(key16_ref[pl.ds(s0, kb), :]).astype(I16)
            tiles = [hit[r:r + rows16] for r in range(0, kb, rows16)]
            chains = [functools.reduce(lax.add, tiles[c::COUNT_CHAINS]) for c in range(COUNT_CHAINS)]
            return cnt + functools.reduce(lax.add, chains)

        cnt = lax.fori_loop(0, n_blocks, body, jnp.zeros((rows16, t), I16))
        return jnp.sum(cnt.astype(I32), axis=0, keepdims=True)

    def count_ge_hi(cand):
        cand_hi = jnp.right_shift(cand, 16).astype(I16)
        return count16(lambda k16: k16 >= cand_hi)

    c0 = count_ge_hi(jnp.zeros((1, t), I32))
    tau = jnp.where(c0 >= top_k, 0, INT_MIN).astype(I32)
    n_sel = jnp.where(c0 >= top_k, c0, 0)

    def n_pending(n_sel):
        return jnp.max(jnp.where((qpos >= top_k) & (n_sel != top_k), 1, 0))

    def bit_search(counter, first, last, carry):
        def cond(c):
            return (c[0] < last) & (c[3] > 0)

        def body(c):
            bi, tau, n_sel, _ = c
            for _ in range(BIT_STEPS_PER_CHECK):
                cand = tau | jnp.left_shift(jnp.int32(1), jnp.maximum(30 - bi, 0))
                n = counter(cand)
                keep = (n >= top_k) & (bi < last)
                tau = jnp.where(keep, cand, tau)
                n_sel = jnp.where(keep, n, n_sel)
                bi = bi + 1
            return bi, tau, n_sel, n_pending(n_sel)

        _, tau, n_sel, pending = lax.while_loop(cond, body, (jnp.int32(first),) + carry)
        return tau, n_sel, pending

    carry = bit_search(count_ge_hi, 0, HI_BIT_STEPS, (tau, n_sel, n_pending(n_sel)))

    tau_hi = jnp.right_shift(carry[0], 16)
    tau_hi16 = tau_hi.astype(I16)
    n_above = count16(lambda k16: k16 > tau_hi16)

    @pl.when(carry[2] > 0)
    def _():
        def build_low(b, c):
            s0 = pl.multiple_of(b * kb, kb)
            kk = keyt_ref[pl.ds(s0, kb), :]
            low = (kk & 0xFFFF) - 32768
            key16_ref[pl.ds(s0, kb), :] = jnp.where(jnp.right_shift(kk, 16) == tau_hi, low, -32768).astype(I16)
            return c

        lax.fori_loop(0, n_blocks, build_low, 0)

    def count_ge_lo(cand):
        cand_lo = ((cand & 0xFFFF) - 32768).astype(I16)
        return n_above + count16(lambda k16: k16 >= cand_lo)

    tau, n_sel, _ = bit_search(count_ge_lo, HI_BIT_STEPS, 31, carry)

    @pl.when(jnp.max(n_sel) > top_k)
    def _():
        need = top_k - count_ge(tau + 1)

        def idx_body(bi, cut):
            cand = cut | jnp.left_shift(jnp.int32(1), idx_bits - 1 - bi)
            n = count_where(lambda kk, krow: (kk == tau) & (krow < cand))
            return jnp.where(n < need, cand, cut)

        cut = lax.fori_loop(0, idx_bits, idx_body, jnp.zeros((1, t), I32))
        cut = jnp.where(n_sel > top_k, cut, seq_len)

        def demote(b, carry):
            s0 = pl.multiple_of(b * kb, kb)
            kk = keyt_ref[pl.ds(s0, kb), :]
            keyt_ref[pl.ds(s0, kb), :] = jnp.where((kk == tau) & (s0 + krow_local > cut), tau - 1, kk)
            return carry

        lax.fori_loop(0, n_blocks, demote, 0)

    tau = jnp.maximum(tau, INT_MIN + 1)

    m_ref[...] = jnp.full(m_ref.shape, MASKED_SCORE, F32)
    acc_ref[...] = jnp.zeros(acc_ref.shape, F32)
    ones = jnp.ones((kb, HEAD_DIM), BF16)

    def att_block(b, carry):
        s0 = pl.multiple_of(b * kb, kb)
        neg = jnp.where(keyt_ref[pl.ds(s0, kb), :] >= tau, 0.0, MASKED_SCORE).T
        logits = [_dot_nt(qst_ref[g * rep * t:(g + 1) * rep * t, :],
                          k_ref[0, pl.ds(s0, kb), g * HEAD_DIM:(g + 1) * HEAD_DIM]) for g in range(N_KV_HEADS)]
        for g in range(N_KV_HEADS):
            gs = slice(g * HEAD_DIM, (g + 1) * HEAD_DIM)
            s = logits[g]
            v1 = jnp.concatenate([v_ref[0, pl.ds(s0, kb), gs], ones], axis=1)
            ps = []
            alphas = []
            for r in range(rep):
                hrows = slice((g * rep + r) * t, (g * rep + r + 1) * t)
                sr = s[r * t:(r + 1) * t] + neg
                tiles = [sr[:, c * LANES:(c + 1) * LANES] for c in range(kb // LANES)]
                mx = functools.reduce(jnp.maximum, tiles)
                m_old = m_ref[hrows, :]
                m_new = jnp.maximum(m_old, jnp.max(mx, axis=1, keepdims=True))
                m_ref[hrows, :] = m_new
                alphas.append(jnp.exp2(m_old - m_new))
                ps.append(jnp.concatenate([jnp.exp2(x - m_new) for x in tiles], axis=1).astype(BF16))
            pv = _dot(jnp.concatenate(ps, axis=0), v1)
            for r in range(rep):
                hrows = slice((g * rep + r) * t, (g * rep + r + 1) * t)
                for c in range(2):
                    cs = slice(c * HEAD_DIM, (c + 1) * HEAD_DIM)
                    acc_ref[hrows, cs] = alphas[r] * acc_ref[hrows, cs] + pv[r * t:(r + 1) * t, cs]
        return carry

    lax.fori_loop(0, n_blocks, att_block, 0)

    for h in range(n_heads):
        hrows = slice(h * t, (h + 1) * t)
        o = acc_ref[hrows, 0:HEAD_DIM] / acc_ref[hrows, HEAD_DIM:2 * HEAD_DIM]
        o_ref[0, :, h * HEAD_DIM:(h + 1) * HEAD_DIM] = o.astype(BF16)


def _dsa_attend(p, ike, iko, iw, batch, seq_len, d):
    n_heads = d // HEAD_DIM
    kv_cols = N_KV_HEADS * HEAD_DIM
    iq_cols = IDX_HEADS * IDX_DIM
    t = min(DSA_TILE, seq_len)
    kb = min(DSA_KEY_BLOCK, seq_len)
    top_k = min(TOP_K_MAX, seq_len // 4)
    assert (d + 2 * kv_cols) % iq_cols == 0 and d % kv_cols == 0 and t % LANES == 0 and kb % t == 0
    p3 = p.reshape(batch, seq_len, p.shape[-1])
    whole = functools.partial(pl.BlockSpec, pipeline_mode=pl.Buffered(1))
    out = pl.pallas_call(
        functools.partial(_dsa_kernel, top_k=top_k, n_heads=n_heads, kb=kb),
        grid=(batch, seq_len // t),
        in_specs=[
            pl.BlockSpec((1, t, d), lambda b, i: (b, i, 0)),
            pl.BlockSpec((1, t, iq_cols), lambda b, i: (b, i, (d + 2 * kv_cols) // iq_cols)),
            whole((1, seq_len, kv_cols), lambda b, i: (b, 0, d // kv_cols)),
            whole((1, seq_len, kv_cols), lambda b, i: (b, 0, d // kv_cols + 1)),
            whole((1, seq_len, LANES), lambda b, i: (b, 0, 0)),
            whole((1, seq_len, LANES), lambda b, i: (b, 0, 0)),
            pl.BlockSpec((1, t, LANES), lambda b, i: (b, i, 0)),
        ],
        out_specs=pl.BlockSpec((1, t, d), lambda b, i: (b, i, 0)),
        out_shape=jax.ShapeDtypeStruct((batch, seq_len, d), BF16),
        scratch_shapes=[
            pltpu.VMEM((seq_len, t), I32),
            pltpu.VMEM((seq_len, t), I16),
            pltpu.VMEM((n_heads * t, HEAD_DIM), BF16),
            pltpu.VMEM((LANES, t), F32),
            pltpu.VMEM((n_heads * t, LANES), F32),
            pltpu.VMEM((n_heads * t, 2 * HEAD_DIM), F32),
        ],
        compiler_params=_params("parallel", "arbitrary"),
        name="dsa_attend",
    )(p3, p3, p3, p3,
      ike.reshape(batch, seq_len, LANES), iko.reshape(batch, seq_len, LANES),
      iw.reshape(batch, seq_len, LANES))
    return out.reshape(batch * seq_len, d)


def _proj_res_kernel(a_ref, w_ref, b_ref, x_ref, o_ref):
    o_ref[...] = x_ref[...] + _dot(a_ref[...], w_ref[...]) + b_ref[...]


def _proj_residual(a, w, b, x):
    m, d = x.shape
    kdim = a.shape[1]
    tm = min(ROW_TILE, m)
    return pl.pallas_call(
        _proj_res_kernel,
        grid=(m // tm,),
        in_specs=[
            pl.BlockSpec((tm, kdim), lambda i: (i, 0)),
            pl.BlockSpec((kdim, d), lambda i: (0, 0)),
            pl.BlockSpec((1, d), lambda i: (0, 0)),
            pl.BlockSpec((tm, d), lambda i: (i, 0)),
        ],
        out_specs=pl.BlockSpec((tm, d), lambda i: (i, 0)),
        out_shape=jax.ShapeDtypeStruct((m, d), F32),
        compiler_params=_params("parallel"),
        name="proj_residual",
    )(a, w, b.reshape(1, d), x)


def _glu_kernel(x_ref, g_ref, wa_ref, wg_ref, ba_ref, bg_ref, o_ref, h_ref):
    @pl.when(pl.program_id(1) == 0)
    def _():
        h_ref[...] = _rms(x_ref[...], g_ref[...]).astype(BF16)

    tm = h_ref.shape[0]
    rc = min(PROJ_ROW_CHUNK, tm)
    for r0 in range(0, tm, rc):
        h = h_ref[r0:r0 + rc, :]
        a = _dot(h, wa_ref[...]) + ba_ref[...]
        gate = _dot(h, wg_ref[...]) + bg_ref[...]
        o_ref[r0:r0 + rc, :] = a * jax.nn.sigmoid(gate)


def _glu(x, g, w, b):
    m, d = x.shape
    tm, tn = min(MLP_ROW_TILE, m), min(COL_TILE, d)
    nb = d // tn
    b2 = b.reshape(1, 2 * d)
    return pl.pallas_call(
        _glu_kernel,
        grid=(m // tm, nb),
        in_specs=[
            pl.BlockSpec((tm, d), lambda i, j: (i, 0)),
            pl.BlockSpec((1, d), lambda i, j: (0, 0)),
            pl.BlockSpec((d, tn), lambda i, j: (0, j)),
            pl.BlockSpec((d, tn), lambda i, j: (0, j + nb)),
            pl.BlockSpec((1, tn), lambda i, j: (0, j)),
            pl.BlockSpec((1, tn), lambda i, j: (0, j + nb)),
        ],
        out_specs=pl.BlockSpec((tm, tn), lambda i, j: (i, j)),
        out_shape=jax.ShapeDtypeStruct((m, d), F32),
        scratch_shapes=[pltpu.VMEM((tm, d), BF16)],
        compiler_params=_params("parallel", "arbitrary"),
        name="conv_glu",
    )(x, g.reshape(1, d), w, w, b2, b2)


def _dwconv_kernel(u_ref, uh_ref, w_ref, b_ref, lg_ref, lb_ref, o_ref, sh_ref, c_ref, *, tiles_per_seq):
    i = pl.program_id(0)
    tm, d = u_ref.shape
    rows = tm + CONV_HALO
    halo = jnp.where(i % tiles_per_seq == 0, 0.0, uh_ref[...])
    off = CONV_HALO - (CONV_WIDTH - 1)
    cw = sh_ref.shape[2]
    rc = min(CONV_ROW_CHUNK, tm)
    for c0 in range(0, d, cw):
        cols = slice(c0, c0 + cw)
        ext = jnp.concatenate([halo[:, cols], u_ref[:, cols]], axis=0)
        sh_ref[0] = ext
        for j in range(1, SUBLANES):
            sh_ref[j] = pltpu.roll(ext, rows - j, 0)
        for r0 in range(0, tm, rc):
            acc = jnp.zeros((rc, cw), F32)
            for k in range(CONV_WIDTH):
                j = (off + k) % SUBLANES
                a = off + k - j + r0
                acc = acc + sh_ref[j, a:a + rc, :] * w_ref[k:k + 1, cols]
            c_ref[r0:r0 + rc, cols] = acc + b_ref[:, cols]
    c = c_ref[...]
    mu = jnp.mean(c, axis=-1, keepdims=True)
    cc = c - mu
    var = jnp.mean(cc * cc, axis=-1, keepdims=True)
    y = cc * lax.rsqrt(var + EPS) * lg_ref[...] + lb_ref[...]
    o_ref[...] = (y * jax.nn.sigmoid(y)).astype(BF16)


def _dwconv(u, w_dw, b_dw, ln_g, ln_b, seq_len):
    m, d = u.shape
    tm = min(CONV_ROW_TILE, seq_len)
    halo_blocks = tm // CONV_HALO
    row = lambda v: v.reshape(1, d)
    return pl.pallas_call(
        functools.partial(_dwconv_kernel, tiles_per_seq=seq_len // tm),
        grid=(m // tm,),
        in_specs=[
            pl.BlockSpec((tm, d), lambda i: (i, 0)),
            pl.BlockSpec((CONV_HALO, d), lambda i: (jnp.maximum(i * halo_blocks - 1, 0), 0)),
            pl.BlockSpec((CONV_WIDTH, d), lambda i: (0, 0)),
            pl.BlockSpec((1, d), lambda i: (0, 0)),
            pl.BlockSpec((1, d), lambda i: (0, 0)),
            pl.BlockSpec((1, d), lambda i: (0, 0)),
        ],
        out_specs=pl.BlockSpec((tm, d), lambda i: (i, 0)),
        out_shape=jax.ShapeDtypeStruct((m, d), BF16),
        scratch_shapes=[pltpu.VMEM((SUBLANES, tm + CONV_HALO, min(COL_TILE, d)), F32), pltpu.VMEM((tm, d), F32)],
        compiler_params=_params("parallel"),
        name="conv_dw_ln",
    )(u, u, w_dw, row(b_dw), row(ln_g), row(ln_b))


def kernel(x, norm_mix, norm_mlp, mlp_up, mlp_down, pool_w, pool_scale, dsa_w_in, dsa_w_out,
           conv_w_pw1, conv_b_pw1, conv_w_dw, conv_b_dw, conv_ln_g, conv_ln_b, conv_w_pw2,
           conv_b_pw2, norm_final):
    batch, seq_len, d = x.shape
    depth = norm_mix.shape[0]
    xf = x.reshape(batch * seq_len, d)
    for i in range(depth):
        kind, j = i % N_MIXERS, i // N_MIXERS
        if kind == 0:
            xf = _pool_layer(xf, norm_mix[i], pool_w[j].astype(BF16), pool_scale[j], seq_len)
        elif kind == 1:
            p, ike, iko, iw = _dsa_project(xf, norm_mix[i], dsa_w_in[j], seq_len)
            o = _dsa_attend(p, ike, iko, iw, batch, seq_len, d)
            xf = _proj_residual(o, dsa_w_out[j].astype(BF16), jnp.zeros((d,), F32), xf)
        else:
            u = _glu(xf, norm_mix[i], conv_w_pw1[j].astype(BF16), conv_b_pw1[j])
            u = _dwconv(u, conv_w_dw[j], conv_b_dw[j], conv_ln_g[j], conv_ln_b[j], seq_len)
            xf = _proj_residual(u, conv_w_pw2[j].astype(BF16), conv_b_pw2[j], xf)
        xf = _mlp(xf, norm_mlp[i], mlp_up[i].astype(BF16), mlp_down[i].astype(BF16),
                  norm_final, final_norm=(i == depth - 1))
    return xf.reshape(batch, seq_len, d)
```

```python
import functools

import jax
import jax.numpy as jnp
from jax import lax
from jax.experimental import pallas as pl
from jax.experimental.pallas import tpu as pltpu

F32 = jnp.float32
BF16 = jnp.bfloat16
I32 = jnp.int32
I16 = jnp.int16

N_MIXERS = 3
POOL_WINDOWS = (2, 4, 8, 16)
HEAD_DIM = 128
N_KV_HEADS = 4
IDX_HEADS = 16
IDX_DIM = 64
TOP_K_MAX = 256
ROPE_THETA = 10000.0
CONV_WIDTH = 31
EPS = 1e-6

LANES = 128
SUBLANES = 8
VMEM_LIMIT_BYTES = 56 * 1024 * 1024

ROW_TILE = 512
MLP_ROW_TILE = 1024
FF_TILE = 512
COL_TILE = 512
DSA_TILE = 256
DSA_KEY_BLOCK = 512
COUNT_CHAINS = 8
BIT_STEPS_PER_CHECK = 4
HI_BIT_STEPS = 15
PROJ_ROW_CHUNK = 256
CONV_ROW_TILE = 128
CONV_ROW_CHUNK = 32
POOL_HALO = 16
CONV_HALO = 32

INT_MIN = -(2 ** 31)
MASKED_SCORE = -1e30
LOG2_E = 1.4426950408889634


def _params(*sem):
    return pltpu.CompilerParams(dimension_semantics=sem, vmem_limit_bytes=VMEM_LIMIT_BYTES)


def _rms(x, g):
    ms = jnp.mean(x * x, axis=-1, keepdims=True)
    return x * lax.rsqrt(ms + EPS) * g


def _dot(a, b):
    return jnp.dot(a, b, preferred_element_type=F32)


def _dot_nt(a, b):
    return lax.dot_general(a, b, (((1,), (1,)), ((), ())), preferred_element_type=F32)


def _mlp_kernel(x_ref, g_ref, up_ref, down_ref, fg_ref, o_ref, h_ref, *, final_norm):
    j = pl.program_id(1)

    @pl.when(j == 0)
    def _():
        x = x_ref[...]
        h_ref[...] = _rms(x, g_ref[...]).astype(BF16)
        o_ref[...] = x

    a = jnp.maximum(_dot(h_ref[...], up_ref[...]), 0.0)
    o_ref[...] += _dot((a * a).astype(BF16), down_ref[...])

    if final_norm:
        @pl.when(j == pl.num_programs(1) - 1)
        def _():
            o_ref[...] = _rms(o_ref[...], fg_ref[...])


def _mlp(x, g, up, down, layer, final_g, final_norm):
    m, d = x.shape
    f = up.shape[2]
    tm, tf = min(MLP_ROW_TILE, m), min(FF_TILE, f)
    return pl.pallas_call(
        functools.partial(_mlp_kernel, final_norm=final_norm),
        grid=(m // tm, f // tf),
        in_specs=[
            pl.BlockSpec((tm, d), lambda i, j: (i, 0)),
            pl.BlockSpec((1, d), lambda i, j: (0, 0)),
            pl.BlockSpec((None, d, tf), lambda i, j: (layer, 0, j)),
            pl.BlockSpec((None, tf, d), lambda i, j: (layer, j, 0)),
            pl.BlockSpec((1, d), lambda i, j: (0, 0)),
        ],
        out_specs=pl.BlockSpec((tm, d), lambda i, j: (i, 0)),
        out_shape=jax.ShapeDtypeStruct((m, d), F32),
        scratch_shapes=[pltpu.VMEM((tm, d), BF16)],
        compiler_params=_params("parallel", "arbitrary"),
        name="mlp",
    )(x, g.reshape(1, d), up, down, final_g.reshape(1, d))


def _pool_kernel(x_ref, xh_ref, g_ref, w_ref, sc_ref, o_ref, ext_ref, *, tiles_per_seq):
    i = pl.program_id(0)
    tm, d = x_ref.shape
    c = d // len(POOL_WINDOWS)
    seq_tile = i % tiles_per_seq
    x = x_ref[...]
    g = g_ref[...]
    h = _rms(x, g)
    ext_ref[0:POOL_HALO, :] = jnp.where(seq_tile == 0, 0.0, _rms(xh_ref[...], g))
    ext_ref[POOL_HALO:, :] = h
    pos = (seq_tile * tm + lax.broadcasted_iota(I32, (tm, 1), 0)).astype(F32)
    for gi, w in enumerate(POOL_WINDOWS):
        cols = slice(gi * c, (gi + 1) * c)
        acc = ext_ref[POOL_HALO:POOL_HALO + tm, cols]
        for j in range(1, w):
            acc = acc + ext_ref[POOL_HALO - j:POOL_HALO - j + tm, cols]
        y = acc / jnp.minimum(pos + 1.0, float(w)) - h[:, cols]
        z = _dot(y.astype(BF16), w_ref[gi])
        o_ref[:, cols] = x[:, cols] + z * sc_ref[:, cols]


def _pool_layer(x, g, w, scale, seq_len):
    m, d = x.shape
    tm = min(ROW_TILE, seq_len)
    ng, c, _ = w.shape
    halo_blocks = tm // POOL_HALO
    return pl.pallas_call(
        functools.partial(_pool_kernel, tiles_per_seq=seq_len // tm),
        grid=(m // tm,),
        in_specs=[
            pl.BlockSpec((tm, d), lambda i: (i, 0)),
            pl.BlockSpec((POOL_HALO, d), lambda i: (jnp.maximum(i * halo_blocks - 1, 0), 0)),
            pl.BlockSpec((1, d), lambda i: (0, 0)),
            pl.BlockSpec((ng, c, c), lambda i: (0, 0, 0)),
            pl.BlockSpec((1, d), lambda i: (0, 0)),
        ],
        out_specs=pl.BlockSpec((tm, d), lambda i: (i, 0)),
        out_shape=jax.ShapeDtypeStruct((m, d), F32),
        scratch_shapes=[pltpu.VMEM((tm + POOL_HALO, d), F32)],
        compiler_params=_params("parallel"),
        name="pool_mixer",
    )(x, x, g.reshape(1, d), w, scale.reshape(1, d))


def _rope_full(x, cos, sin_signed):
    return x * cos + pltpu.roll(x, HEAD_DIM // 2, 1) * sin_signed


def _rope_idx(x, cos, sin_signed, first_half):
    q = IDX_DIM // 2
    rot = jnp.where(first_half, pltpu.roll(x, LANES - q, 1), pltpu.roll(x, q, 1))
    return x * cos + rot * sin_signed


def _proj_kernel(x_ref, g_ref, w_ref, wt_ref, c128_ref, s128_ref, c64_ref, s64_ref,
                 p_ref, ike_ref, iko_ref, iw_ref, h_ref, *, n_q_blocks, n_rope_blocks, n_main, q_scale, iw_scale):
    j = pl.program_id(1)
    tm = x_ref.shape[0]
    cb = w_ref.shape[1]
    lane = lax.broadcasted_iota(I32, (tm, LANES), 1)
    first_half = (lane % IDX_DIM) < (IDX_DIM // 2)
    rc = min(PROJ_ROW_CHUNK, tm)
    row_chunks = [slice(r0, r0 + rc) for r0 in range(0, tm, rc)]
    first_half_chunk = (lax.broadcasted_iota(I32, (rc, LANES), 1) % IDX_DIM) < (IDX_DIM // 2)

    @pl.when(j == 0)
    def _():
        h_ref[...] = _rms(x_ref[...], g_ref[...]).astype(BF16)

    @pl.when(j < n_rope_blocks)
    def _():
        factor = jnp.where(j < n_q_blocks, q_scale, 1.0)
        for rs in row_chunks:
            r = _dot(h_ref[rs, :], w_ref[...]) * factor
            cos, sin = c128_ref[rs, :], s128_ref[rs, :]
            for c in range(cb // LANES):
                cs = slice(c * LANES, (c + 1) * LANES)
                p_ref[rs, cs] = _rope_full(r[:, cs], cos, sin).astype(BF16)

    @pl.when(j == n_rope_blocks)
    def _():
        p_ref[...] = _dot(h_ref[...], w_ref[...]).astype(BF16)

    @pl.when((j > n_rope_blocks) & (j < n_main))
    def _():
        for rs in row_chunks:
            r = _dot(h_ref[rs, :], w_ref[...])
            cos, sin = c64_ref[rs, :], s64_ref[rs, :]
            for c in range(cb // LANES):
                cs = slice(c * LANES, (c + 1) * LANES)
                p_ref[rs, cs] = _rope_idx(r[:, cs], cos, sin, first_half_chunk).astype(BF16)

    @pl.when(j == n_main)
    def _():
        r = _dot(h_ref[...], wt_ref[...])
        roped = _rope_idx(r, c64_ref[...], s64_ref[...], first_half)
        ike = jnp.where(lane < IDX_DIM, roped, 0.0)
        ike_ref[...] = ike.astype(BF16)
        iko_ref[...] = pltpu.roll(ike, IDX_DIM, 1).astype(BF16)
        iw_ref[...] = r * iw_scale


def _rope_tables(seq_len):
    pos = jnp.arange(seq_len, dtype=F32)[:, None]

    def tables(dim):
        inv = ROPE_THETA ** (-jnp.arange(0, dim, 2, dtype=F32) / dim)
        ang = pos * inv[None, :]
        cos, sin = jnp.cos(ang), jnp.sin(ang)
        reps = LANES // dim
        return (jnp.tile(jnp.concatenate([cos, cos], axis=1), (1, reps)),
                jnp.tile(jnp.concatenate([-sin, sin], axis=1), (1, reps)))

    return tables(HEAD_DIM) + tables(IDX_DIM)


def _dsa_project(x, g, w_in, seq_len):
    m, d = x.shape
    q_cols = d
    kv_cols = N_KV_HEADS * HEAD_DIM
    iq_cols = IDX_HEADS * IDX_DIM
    main_cols = q_cols + 2 * kv_cols + iq_cols
    cb = min(COL_TILE, kv_cols)
    assert kv_cols % cb == 0 and q_cols % cb == 0 and iq_cols % cb == 0 and kv_cols == cb
    n_rope_blocks = (q_cols + kv_cols) // cb
    n_main = main_cols // cb
    assert IDX_DIM + IDX_HEADS <= LANES
    w_main = w_in[:, :main_cols].astype(BF16)
    w_tail = jnp.pad(w_in[:, main_cols:], ((0, 0), (0, LANES - IDX_DIM - IDX_HEADS))).astype(BF16)
    c128, s128, c64, s64 = _rope_tables(seq_len)
    tm = min(MLP_ROW_TILE, seq_len)
    tps = seq_len // tm
    tab = pl.BlockSpec((tm, LANES), lambda i, j: (i % tps, 0))
    tail = pl.BlockSpec((tm, LANES), lambda i, j: (i, 0))
    return pl.pallas_call(
        functools.partial(_proj_kernel, n_q_blocks=q_cols // cb, n_rope_blocks=n_rope_blocks, n_main=n_main,
                          q_scale=(HEAD_DIM ** -0.5) * LOG2_E, iw_scale=(IDX_HEADS ** -0.5) * (IDX_DIM ** -0.5)),
        grid=(m // tm, n_main + 1),
        in_specs=[
            pl.BlockSpec((tm, d), lambda i, j: (i, 0)),
            pl.BlockSpec((1, d), lambda i, j: (0, 0)),
            pl.BlockSpec((d, cb), lambda i, j: (0, jnp.minimum(j, n_main - 1))),
            pl.BlockSpec((d, LANES), lambda i, j: (0, 0)),
            tab, tab, tab, tab,
        ],
        out_specs=[
            pl.BlockSpec((tm, cb), lambda i, j: (i, jnp.minimum(j, n_main - 1))),
            tail, tail, tail,
        ],
        out_shape=[
            jax.ShapeDtypeStruct((m, main_cols), BF16),
            jax.ShapeDtypeStruct((m, LANES), BF16),
            jax.ShapeDtypeStruct((m, LANES), BF16),
            jax.ShapeDtypeStruct((m, LANES), F32),
        ],
        scratch_shapes=[pltpu.VMEM((tm, d), BF16)],
        compiler_params=_params("parallel", "arbitrary"),
        name="dsa_project",
    )(x, g.reshape(1, d), w_main, w_tail, c128, s128, c64, s64)


def _dsa_kernel(q_ref, iq_ref, k_ref, v_ref, ike_ref, iko_ref, iw_ref, o_ref,
                keyt_ref, key16_ref, qst_ref, iwt_ref, m_ref, acc_ref, *, top_k, n_heads, kb):
    i = pl.program_id(1)
    t = q_ref.shape[1]
    seq_len = k_ref.shape[1]
    rep = n_heads // N_KV_HEADS
    n_blocks = (i * t) // kb + 1
    idx_bits = (seq_len - 1).bit_length()

    for h in range(n_heads):
        qst_ref[h * t:(h + 1) * t, :] = q_ref[0, :, h * HEAD_DIM:(h + 1) * HEAD_DIM]
    iwt_ref[...] = iw_ref[0].T

    krow_local = lax.broadcasted_iota(I32, (kb, t), 0)
    qpos = i * t + lax.broadcasted_iota(I32, (1, t), 1)

    def score_block(b, carry):
        s0 = pl.multiple_of(b * kb, kb)
        ike = ike_ref[0, pl.ds(s0, kb), :]
        iko = iko_ref[0, pl.ds(s0, kb), :]
        dots = []
        for p in range(IDX_HEADS // 2):
            iqp = iq_ref[0, :, p * LANES:(p + 1) * LANES]
            dots += [_dot_nt(ike, iqp), _dot_nt(iko, iqp)]
        sc = jnp.zeros((kb, t), F32)
        for h, d in enumerate(dots):
            sc = sc + iwt_ref[IDX_DIM + h:IDX_DIM + h + 1, :] * jnp.maximum(d, 0.0)
        bits = lax.bitcast_convert_type(sc, I32)
        key = jnp.where(bits < 0, bits ^ 0x7FFFFFFF, bits)
        key = jnp.where(s0 + krow_local <= qpos, key, INT_MIN)
        keyt_ref[pl.ds(s0, kb), :] = key
        key16_ref[pl.ds(s0, kb), :] = jnp.right_shift(key, 16).astype(I16)
        return carry

    lax.fori_loop(0, n_blocks, score_block, 0)

    def count_where(pred):
        def body(b, cnt):
            s0 = pl.multiple_of(b * kb, kb)
            hit = pred(keyt_ref[pl.ds(s0, kb), :], s0 + krow_local).astype(I32)
            part = jnp.sum(hit.reshape(COUNT_CHAINS, kb // (COUNT_CHAINS * SUBLANES), SUBLANES, t), axis=1)
            return cnt + jnp.sum(part, axis=0)

        cnt = lax.fori_loop(0, n_blocks, body, jnp.zeros((SUBLANES, t), I32))
        return jnp.sum(cnt, axis=0, keepdims=True)

    def count_ge(cand):
        return count_where(lambda kk, krow: kk >= cand)

    rows16 = 2 * SUBLANES

    def count16(pred):
        def body(b, cnt):
            s0 = pl.multiple_of(b * kb, kb)
            hit = pred(key16_ref[pl.ds(s0, kb), :]).astype(I16)
            tiles = [hit[r:r + rows16] for r in range(0, kb, rows16)]
            chains = [functools.reduce(lax.add, tiles[c::COUNT_CHAINS]) for c in range(COUNT_CHAINS)]
            return cnt + functools.reduce(lax.add, chains)

        cnt = lax.fori_loop(0, n_blocks, body, jnp.zeros((rows16, t), I16))
        return jnp.sum(cnt.astype(I32), axis=0, keepdims=True)

    def count_ge_hi(cand):
        cand_hi = jnp.right_shift(cand, 16).astype(I16)
        return count16(lambda k16: k16 >= cand_hi)

    c0 = count_ge_hi(jnp.zeros((1, t), I32))
    tau = jnp.where(c0 >= top_k, 0, INT_MIN).astype(I32)
    n_sel = jnp.where(c0 >= top_k, c0, 0)

    def n_pending(n_sel):
        return jnp.max(jnp.where((qpos >= top_k) & (n_sel != top_k), 1, 0))

    def bit_search(counter, first, last, carry):
        def cond(c):
            return (c[0] < last) & (c[3] > 0)

        def body(c):
            bi, tau, n_sel, _ = c
            for _ in range(BIT_STEPS_PER_CHECK):
                cand = tau | jnp.left_shift(jnp.int32(1), jnp.maximum(30 - bi, 0))
                n = counter(cand)
                keep = (n >= top_k) & (bi < last)
                tau = jnp.where(keep, cand, tau)
                n_sel = jnp.where(keep, n, n_sel)
                bi = bi + 1
            return bi, tau, n_sel, n_pending(n_sel)

        _, tau, n_sel, pending = lax.while_loop(cond, body, (jnp.int32(first),) + carry)
        return tau, n_sel, pending

    carry = bit_search(count_ge_hi, 0, HI_BIT_STEPS, (tau, n_sel, n_pending(n_sel)))

    tau_hi = jnp.right_shift(carry[0], 16)
    tau_hi16 = tau_hi.astype(I16)
    n_above = count16(lambda k16: k16 > tau_hi16)

    @pl.when(carry[2] > 0)
    def _():
        def build_low(b, c):
            s0 = pl.multiple_of(b * kb, kb)
            kk = keyt_ref[pl.ds(s0, kb), :]
            low = (kk & 0xFFFF) - 32768
            key16_ref[pl.ds(s0, kb), :] = jnp.where(jnp.right_shift(kk, 16) == tau_hi, low, -32768).astype(I16)
            return c

        lax.fori_loop(0, n_blocks, build_low, 0)

    def count_ge_lo(cand):
        cand_lo = ((cand & 0xFFFF) - 32768).astype(I16)
        return n_above + count16(lambda k16: k16 >= cand_lo)

    tau, n_sel, _ = bit_search(count_ge_lo, HI_BIT_STEPS, 31, carry)

    @pl.when(jnp.max(n_sel) > top_k)
    def _():
        need = top_k - count_ge(tau + 1)

        def idx_body(bi, cut):
            cand = cut | jnp.left_shift(jnp.int32(1), idx_bits - 1 - bi)
            n = count_where(lambda kk, krow: (kk == tau) & (krow < cand))
            return jnp.where(n < need, cand, cut)

        cut = lax.fori_loop(0, idx_bits, idx_body, jnp.zeros((1, t), I32))
        cut = jnp.where(n_sel > top_k, cut, seq_len)

        def demote(b, carry):
            s0 = pl.multiple_of(b * kb, kb)
            kk = keyt_ref[pl.ds(s0, kb), :]
            keyt_ref[pl.ds(s0, kb), :] = jnp.where((kk == tau) & (s0 + krow_local > cut), tau - 1, kk)
            return carry

        lax.fori_loop(0, n_blocks, demote, 0)

    tau = jnp.maximum(tau, INT_MIN + 1)

    m_ref[...] = jnp.full(m_ref.shape, MASKED_SCORE, F32)
    acc_ref[...] = jnp.zeros(acc_ref.shape, F32)
    ones = jnp.ones((kb, HEAD_DIM), BF16)

    def att_block(b, carry):
        s0 = pl.multiple_of(b * kb, kb)
        neg = jnp.where(keyt_ref[pl.ds(s0, kb), :] >= tau, 0.0, MASKED_SCORE).T
        logits = [_dot_nt(qst_ref[g * rep * t:(g + 1) * rep * t, :],
                          k_ref[0, pl.ds(s0, kb), g * HEAD_DIM:(g + 1) * HEAD_DIM]) for g in range(N_KV_HEADS)]
        for g in range(N_KV_HEADS):
            gs = slice(g * HEAD_DIM, (g + 1) * HEAD_DIM)
            s = logits[g]
            v1 = jnp.concatenate([v_ref[0, pl.ds(s0, kb), gs], ones], axis=1)
            ps = []
            alphas = []
            for r in range(rep):
                hrows = slice((g * rep + r) * t, (g * rep + r + 1) * t)
                sr = s[r * t:(r + 1) * t] + neg
                tiles = [sr[:, c * LANES:(c + 1) * LANES] for c in range(kb // LANES)]
                mx = functools.reduce(jnp.maximum, tiles)
                m_old = m_ref[hrows, :]
                m_new = jnp.maximum(m_old, jnp.max(mx, axis=1, keepdims=True))
                m_ref[hrows, :] = m_new
                alphas.append(jnp.exp2(m_old - m_new))
                ps.append(jnp.concatenate([jnp.exp2(x - m_new) for x in tiles], axis=1).astype(BF16))
            pv = _dot(jnp.concatenate(ps, axis=0), v1)
            for r in range(rep):
                hrows = slice((g * rep + r) * t, (g * rep + r + 1) * t)
                for c in range(2):
                    cs = slice(c * HEAD_DIM, (c + 1) * HEAD_DIM)
                    acc_ref[hrows, cs] = alphas[r] * acc_ref[hrows, cs] + pv[r * t:(r + 1) * t, cs]
        return carry

    lax.fori_loop(0, n_blocks, att_block, 0)

    for h in range(n_heads):
        hrows = slice(h * t, (h + 1) * t)
        o = acc_ref[hrows, 0:HEAD_DIM] / acc_ref[hrows, HEAD_DIM:2 * HEAD_DIM]
        o_ref[0, :, h * HEAD_DIM:(h + 1) * HEAD_DIM] = o.astype(BF16)


def _dsa_attend(p, ike, iko, iw, batch, seq_len, d):
    n_heads = d // HEAD_DIM
    kv_cols = N_KV_HEADS * HEAD_DIM
    iq_cols = IDX_HEADS * IDX_DIM
    t = min(DSA_TILE, seq_len)
    kb = min(DSA_KEY_BLOCK, seq_len)
    top_k = min(TOP_K_MAX, seq_len // 4)
    assert (d + 2 * kv_cols) % iq_cols == 0 and d % kv_cols == 0 and t % LANES == 0 and kb % t == 0
    p3 = p.reshape(batch, seq_len, p.shape[-1])
    whole = functools.partial(pl.BlockSpec, pipeline_mode=pl.Buffered(1))
    out = pl.pallas_call(
        functools.partial(_dsa_kernel, top_k=top_k, n_heads=n_heads, kb=kb),
        grid=(batch, seq_len // t),
        in_specs=[
            pl.BlockSpec((1, t, d), lambda b, i: (b, i, 0)),
            pl.BlockSpec((1, t, iq_cols), lambda b, i: (b, i, (d + 2 * kv_cols) // iq_cols)),
            whole((1, seq_len, kv_cols), lambda b, i: (b, 0, d // kv_cols)),
            whole((1, seq_len, kv_cols), lambda b, i: (b, 0, d // kv_cols + 1)),
            whole((1, seq_len, LANES), lambda b, i: (b, 0, 0)),
            whole((1, seq_len, LANES), lambda b, i: (b, 0, 0)),
            pl.BlockSpec((1, t, LANES), lambda b, i: (b, i, 0)),
        ],
        out_specs=pl.BlockSpec((1, t, d), lambda b, i: (b, i, 0)),
        out_shape=jax.ShapeDtypeStruct((batch, seq_len, d), BF16),
        scratch_shapes=[
            pltpu.VMEM((seq_len, t), I32),
            pltpu.VMEM((seq_len, t), I16),
            pltpu.VMEM((n_heads * t, HEAD_DIM), BF16),
            pltpu.VMEM((LANES, t), F32),
            pltpu.VMEM((n_heads * t, LANES), F32),
            pltpu.VMEM((n_heads * t, 2 * HEAD_DIM), F32),
        ],
        compiler_params=_params("parallel", "arbitrary"),
        name="dsa_attend",
    )(p3, p3, p3, p3,
      ike.reshape(batch, seq_len, LANES), iko.reshape(batch, seq_len, LANES),
      iw.reshape(batch, seq_len, LANES))
    return out.reshape(batch * seq_len, d)


def _proj_res_kernel(a_ref, w_ref, b_ref, x_ref, o_ref):
    o_ref[...] = x_ref[...] + _dot(a_ref[...], w_ref[...]) + b_ref[...]


def _proj_residual(a, w, b, x):
    m, d = x.shape
    kdim = a.shape[1]
    tm = min(ROW_TILE, m)
    return pl.pallas_call(
        _proj_res_kernel,
        grid=(m // tm,),
        in_specs=[
            pl.BlockSpec((tm, kdim), lambda i: (i, 0)),
            pl.BlockSpec((kdim, d), lambda i: (0, 0)),
            pl.BlockSpec((1, d), lambda i: (0, 0)),
            pl.BlockSpec((tm, d), lambda i: (i, 0)),
        ],
        out_specs=pl.BlockSpec((tm, d), lambda i: (i, 0)),
        out_shape=jax.ShapeDtypeStruct((m, d), F32),
        compiler_params=_params("parallel"),
        name="proj_residual",
    )(a, w, b.reshape(1, d), x)


def _glu_kernel(x_ref, g_ref, wa_ref, wg_ref, ba_ref, bg_ref, o_ref, h_ref):
    @pl.when(pl.program_id(1) == 0)
    def _():
        h_ref[...] = _rms(x_ref[...], g_ref[...]).astype(BF16)

    tm = h_ref.shape[0]
    rc = min(PROJ_ROW_CHUNK, tm)
    for r0 in range(0, tm, rc):
        h = h_ref[r0:r0 + rc, :]
        a = _dot(h, wa_ref[...]) + ba_ref[...]
        gate = _dot(h, wg_ref[...]) + bg_ref[...]
        o_ref[r0:r0 + rc, :] = a * jax.nn.sigmoid(gate)


def _glu(x, g, w, b):
    m, d = x.shape
    tm, tn = min(MLP_ROW_TILE, m), min(COL_TILE, d)
    nb = d // tn
    b2 = b.reshape(1, 2 * d)
    return pl.pallas_call(
        _glu_kernel,
        grid=(m // tm, nb),
        in_specs=[
            pl.BlockSpec((tm, d), lambda i, j: (i, 0)),
            pl.BlockSpec((1, d), lambda i, j: (0, 0)),
            pl.BlockSpec((d, tn), lambda i, j: (0, j)),
            pl.BlockSpec((d, tn), lambda i, j: (0, j + nb)),
            pl.BlockSpec((1, tn), lambda i, j: (0, j)),
            pl.BlockSpec((1, tn), lambda i, j: (0, j + nb)),
        ],
        out_specs=pl.BlockSpec((tm, tn), lambda i, j: (i, j)),
        out_shape=jax.ShapeDtypeStruct((m, d), F32),
        scratch_shapes=[pltpu.VMEM((tm, d), BF16)],
        compiler_params=_params("parallel", "arbitrary"),
        name="conv_glu",
    )(x, g.reshape(1, d), w, w, b2, b2)


def _dwconv_kernel(u_ref, uh_ref, w_ref, b_ref, lg_ref, lb_ref, o_ref, sh_ref, c_ref, *, tiles_per_seq):
    i = pl.program_id(0)
    tm, d = u_ref.shape
    rows = tm + CONV_HALO
    halo = jnp.where(i % tiles_per_seq == 0, 0.0, uh_ref[...])
    off = CONV_HALO - (CONV_WIDTH - 1)
    cw = sh_ref.shape[2]
    rc = min(CONV_ROW_CHUNK, tm)
    for c0 in range(0, d, cw):
        cols = slice(c0, c0 + cw)
        ext = jnp.concatenate([halo[:, cols], u_ref[:, cols]], axis=0)
        sh_ref[0] = ext
        for j in range(1, SUBLANES):
            sh_ref[j] = pltpu.roll(ext, rows - j, 0)
        for r0 in range(0, tm, rc):
            acc = jnp.zeros((rc, cw), F32)
            for k in range(CONV_WIDTH):
                j = (off + k) % SUBLANES
                a = off + k - j + r0
                acc = acc + sh_ref[j, a:a + rc, :] * w_ref[k:k + 1, cols]
            c_ref[r0:r0 + rc, cols] = acc + b_ref[:, cols]
    c = c_ref[...]
    mu = jnp.mean(c, axis=-1, keepdims=True)
    cc = c - mu
    var = jnp.mean(cc * cc, axis=-1, keepdims=True)
    y = cc * lax.rsqrt(var + EPS) * lg_ref[...] + lb_ref[...]
    o_ref[...] = (y * jax.nn.sigmoid(y)).astype(BF16)


def _dwconv(u, w_dw, b_dw, ln_g, ln_b, seq_len):
    m, d = u.shape
    tm = min(CONV_ROW_TILE, seq_len)
    halo_blocks = tm // CONV_HALO
    row = lambda v: v.reshape(1, d)
    return pl.pallas_call(
        functools.partial(_dwconv_kernel, tiles_per_seq=seq_len // tm),
        grid=(m // tm,),
        in_specs=[
            pl.BlockSpec((tm, d), lambda i: (i, 0)),
            pl.BlockSpec((CONV_HALO, d), lambda i: (jnp.maximum(i * halo_blocks - 1, 0), 0)),
            pl.BlockSpec((CONV_WIDTH, d), lambda i: (0, 0)),
            pl.BlockSpec((1, d), lambda i: (0, 0)),
            pl.BlockSpec((1, d), lambda i: (0, 0)),
            pl.BlockSpec((1, d), lambda i: (0, 0)),
        ],
        out_specs=pl.BlockSpec((tm, d), lambda i: (i, 0)),
        out_shape=jax.ShapeDtypeStruct((m, d), BF16),
        scratch_shapes=[pltpu.VMEM((SUBLANES, tm + CONV_HALO, min(COL_TILE, d)), F32), pltpu.VMEM((tm, d), F32)],
        compiler_params=_params("parallel"),
        name="conv_dw_ln",
    )(u, u, w_dw, row(b_dw), row(ln_g), row(ln_b))


def kernel(x, norm_mix, norm_mlp, mlp_up, mlp_down, pool_w, pool_scale, dsa_w_in, dsa_w_out,
           conv_w_pw1, conv_b_pw1, conv_w_dw, conv_b_dw, conv_ln_g, conv_ln_b, conv_w_pw2,
           conv_b_pw2, norm_final):
    batch, seq_len, d = x.shape
    depth = norm_mix.shape[0]
    xf = x.reshape(batch * seq_len, d)
    up_bf16, down_bf16 = mlp_up.astype(BF16), mlp_down.astype(BF16)
    for i in range(depth):
        kind, j = i % N_MIXERS, i // N_MIXERS
        if kind == 0:
            xf = _pool_layer(xf, norm_mix[i], pool_w[j].astype(BF16), pool_scale[j], seq_len)
        elif kind == 1:
            p, ike, iko, iw = _dsa_project(xf, norm_mix[i], dsa_w_in[j], seq_len)
            o = _dsa_attend(p, ike, iko, iw, batch, seq_len, d)
            xf = _proj_residual(o, dsa_w_out[j].astype(BF16), jnp.zeros((d,), F32), xf)
        else:
            u = _glu(xf, norm_mix[i], conv_w_pw1[j].astype(BF16), conv_b_pw1[j])
            u = _dwconv(u, conv_w_dw[j], conv_b_dw[j], conv_ln_g[j], conv_ln_b[j], seq_len)
            xf = _proj_residual(u, conv_w_pw2[j].astype(BF16), conv_b_pw2[j], xf)
        xf = _mlp(xf, norm_mlp[i], up_bf16, down_bf16, i, norm_final, final_norm=(i == depth - 1))
    return xf.reshape(batch, seq_len, d)
```

```python
import functools

import jax
import jax.numpy as jnp
from jax import lax
from jax.experimental import pallas as pl
from jax.experimental.pallas import tpu as pltpu

F32 = jnp.float32
BF16 = jnp.bfloat16
I32 = jnp.int32
I16 = jnp.int16

N_MIXERS = 3
POOL_WINDOWS = (2, 4, 8, 16)
HEAD_DIM = 128
N_KV_HEADS = 4
IDX_HEADS = 16
IDX_DIM = 64
TOP_K_MAX = 256
ROPE_THETA = 10000.0
CONV_WIDTH = 31
EPS = 1e-6

LANES = 128
SUBLANES = 8
VMEM_LIMIT_BYTES = 56 * 1024 * 1024

ROW_TILE = 512
MLP_ROW_TILE = 1024
FF_TILE = 512
COL_TILE = 512
DSA_TILE = 256
DSA_KEY_BLOCK = 512
COUNT_CHAINS = 8
BIT_STEPS_PER_CHECK = 4
HI_BIT_STEPS = 15
PROJ_ROW_CHUNK = 256
CONV_ROW_TILE = 128
CONV_ROW_CHUNK = 128
POOL_HALO = 16
CONV_HALO = 32

INT_MIN = -(2 ** 31)
MASKED_SCORE = -1e30
LOG2_E = 1.4426950408889634


def _params(*sem):
    return pltpu.CompilerParams(dimension_semantics=sem, vmem_limit_bytes=VMEM_LIMIT_BYTES)


def _rms(x, g):
    ms = jnp.mean(x * x, axis=-1, keepdims=True)
    return x * lax.rsqrt(ms + EPS) * g


def _dot(a, b):
    return jnp.dot(a, b, preferred_element_type=F32)


def _dot_nt(a, b):
    return lax.dot_general(a, b, (((1,), (1,)), ((), ())), preferred_element_type=F32)


def _mlp_kernel(x_ref, g_ref, up_ref, down_ref, fg_ref, o_ref, h_ref, *, final_norm):
    j = pl.program_id(1)

    @pl.when(j == 0)
    def _():
        x = x_ref[...]
        h_ref[...] = _rms(x, g_ref[...]).astype(BF16)
        o_ref[...] = x

    a = jnp.maximum(_dot(h_ref[...], up_ref[...]), 0.0)
    o_ref[...] += _dot((a * a).astype(BF16), down_ref[...])

    if final_norm:
        @pl.when(j == pl.num_programs(1) - 1)
        def _():
            o_ref[...] = _rms(o_ref[...], fg_ref[...])


def _mlp(x, g, up, down, layer, final_g, final_norm):
    m, d = x.shape
    f = up.shape[2]
    tm, tf = min(MLP_ROW_TILE, m), min(FF_TILE, f)
    return pl.pallas_call(
        functools.partial(_mlp_kernel, final_norm=final_norm),
        grid=(m // tm, f // tf),
        in_specs=[
            pl.BlockSpec((tm, d), lambda i, j: (i, 0)),
            pl.BlockSpec((1, d), lambda i, j: (0, 0)),
            pl.BlockSpec((None, d, tf), lambda i, j: (layer, 0, j)),
            pl.BlockSpec((None, tf, d), lambda i, j: (layer, j, 0)),
            pl.BlockSpec((1, d), lambda i, j: (0, 0)),
        ],
        out_specs=pl.BlockSpec((tm, d), lambda i, j: (i, 0)),
        out_shape=jax.ShapeDtypeStruct((m, d), F32),
        scratch_shapes=[pltpu.VMEM((tm, d), BF16)],
        compiler_params=_params("parallel", "arbitrary"),
        name="mlp",
    )(x, g.reshape(1, d), up, down, final_g.reshape(1, d))


def _pool_kernel(x_ref, xh_ref, g_ref, w_ref, sc_ref, o_ref, ext_ref, *, tiles_per_seq):
    i = pl.program_id(0)
    tm, d = x_ref.shape
    c = d // len(POOL_WINDOWS)
    seq_tile = i % tiles_per_seq
    x = x_ref[...]
    g = g_ref[...]
    h = _rms(x, g)
    ext_ref[0:POOL_HALO, :] = jnp.where(seq_tile == 0, 0.0, _rms(xh_ref[...], g))
    ext_ref[POOL_HALO:, :] = h
    pos = (seq_tile * tm + lax.broadcasted_iota(I32, (tm, 1), 0)).astype(F32)
    for gi, w in enumerate(POOL_WINDOWS):
        cols = slice(gi * c, (gi + 1) * c)
        acc = ext_ref[POOL_HALO:POOL_HALO + tm, cols]
        for j in range(1, w):
            acc = acc + ext_ref[POOL_HALO - j:POOL_HALO - j + tm, cols]
        y = acc / jnp.minimum(pos + 1.0, float(w)) - h[:, cols]
        z = _dot(y.astype(BF16), w_ref[gi])
        o_ref[:, cols] = x[:, cols] + z * sc_ref[:, cols]


def _pool_layer(x, g, w, scale, seq_len):
    m, d = x.shape
    tm = min(ROW_TILE, seq_len)
    ng, c, _ = w.shape
    halo_blocks = tm // POOL_HALO
    return pl.pallas_call(
        functools.partial(_pool_kernel, tiles_per_seq=seq_len // tm),
        grid=(m // tm,),
        in_specs=[
            pl.BlockSpec((tm, d), lambda i: (i, 0)),
            pl.BlockSpec((POOL_HALO, d), lambda i: (jnp.maximum(i * halo_blocks - 1, 0), 0)),
            pl.BlockSpec((1, d), lambda i: (0, 0)),
            pl.BlockSpec((ng, c, c), lambda i: (0, 0, 0)),
            pl.BlockSpec((1, d), lambda i: (0, 0)),
        ],
        out_specs=pl.BlockSpec((tm, d), lambda i: (i, 0)),
        out_shape=jax.ShapeDtypeStruct((m, d), F32),
        scratch_shapes=[pltpu.VMEM((tm + POOL_HALO, d), F32)],
        compiler_params=_params("parallel"),
        name="pool_mixer",
    )(x, x, g.reshape(1, d), w, scale.reshape(1, d))


def _rope_full(x, cos, sin_signed):
    return x * cos + pltpu.roll(x, HEAD_DIM // 2, 1) * sin_signed


def _rope_idx(x, cos, sin_signed, first_half):
    q = IDX_DIM // 2
    rot = jnp.where(first_half, pltpu.roll(x, LANES - q, 1), pltpu.roll(x, q, 1))
    return x * cos + rot * sin_signed


def _proj_kernel(x_ref, g_ref, w_ref, wt_ref, c128_ref, s128_ref, c64_ref, s64_ref,
                 p_ref, ike_ref, iko_ref, iw_ref, h_ref, *, n_q_blocks, n_rope_blocks, n_main, q_scale, iw_scale):
    j = pl.program_id(1)
    tm = x_ref.shape[0]
    cb = w_ref.shape[1]
    lane = lax.broadcasted_iota(I32, (tm, LANES), 1)
    first_half = (lane % IDX_DIM) < (IDX_DIM // 2)
    rc = min(PROJ_ROW_CHUNK, tm)
    row_chunks = [slice(r0, r0 + rc) for r0 in range(0, tm, rc)]
    first_half_chunk = (lax.broadcasted_iota(I32, (rc, LANES), 1) % IDX_DIM) < (IDX_DIM // 2)

    @pl.when(j == 0)
    def _():
        h_ref[...] = _rms(x_ref[...], g_ref[...]).astype(BF16)

    @pl.when(j < n_rope_blocks)
    def _():
        factor = jnp.where(j < n_q_blocks, q_scale, 1.0)
        for rs in row_chunks:
            r = _dot(h_ref[rs, :], w_ref[...]) * factor
            cos, sin = c128_ref[rs, :], s128_ref[rs, :]
            for c in range(cb // LANES):
                cs = slice(c * LANES, (c + 1) * LANES)
                p_ref[rs, cs] = _rope_full(r[:, cs], cos, sin).astype(BF16)

    @pl.when(j == n_rope_blocks)
    def _():
        p_ref[...] = _dot(h_ref[...], w_ref[...]).astype(BF16)

    @pl.when((j > n_rope_blocks) & (j < n_main))
    def _():
        for rs in row_chunks:
            r = _dot(h_ref[rs, :], w_ref[...])
            cos, sin = c64_ref[rs, :], s64_ref[rs, :]
            for c in range(cb // LANES):
                cs = slice(c * LANES, (c + 1) * LANES)
                p_ref[rs, cs] = _rope_idx(r[:, cs], cos, sin, first_half_chunk).astype(BF16)

    @pl.when(j == n_main)
    def _():
        r = _dot(h_ref[...], wt_ref[...])
        roped = _rope_idx(r, c64_ref[...], s64_ref[...], first_half)
        ike = jnp.where(lane < IDX_DIM, roped, 0.0)
        ike_ref[...] = ike.astype(BF16)
        iko_ref[...] = pltpu.roll(ike, IDX_DIM, 1).astype(BF16)
        iw_ref[...] = r * iw_scale


def _rope_tables(seq_len):
    pos = jnp.arange(seq_len, dtype=F32)[:, None]

    def tables(dim):
        inv = ROPE_THETA ** (-jnp.arange(0, dim, 2, dtype=F32) / dim)
        ang = pos * inv[None, :]
        cos, sin = jnp.cos(ang), jnp.sin(ang)
        reps = LANES // dim
        return (jnp.tile(jnp.concatenate([cos, cos], axis=1), (1, reps)),
                jnp.tile(jnp.concatenate([-sin, sin], axis=1), (1, reps)))

    return tables(HEAD_DIM) + tables(IDX_DIM)


def _dsa_project(x, g, w_in, seq_len):
    m, d = x.shape
    q_cols = d
    kv_cols = N_KV_HEADS * HEAD_DIM
    iq_cols = IDX_HEADS * IDX_DIM
    main_cols = q_cols + 2 * kv_cols + iq_cols
    cb = min(COL_TILE, kv_cols)
    assert kv_cols % cb == 0 and q_cols % cb == 0 and iq_cols % cb == 0 and kv_cols == cb
    n_rope_blocks = (q_cols + kv_cols) // cb
    n_main = main_cols // cb
    assert IDX_DIM + IDX_HEADS <= LANES
    w_main = w_in[:, :main_cols].astype(BF16)
    w_tail = jnp.pad(w_in[:, main_cols:], ((0, 0), (0, LANES - IDX_DIM - IDX_HEADS))).astype(BF16)
    c128, s128, c64, s64 = _rope_tables(seq_len)
    tm = min(MLP_ROW_TILE, seq_len)
    tps = seq_len // tm
    tab = pl.BlockSpec((tm, LANES), lambda i, j: (i % tps, 0))
    tail = pl.BlockSpec((tm, LANES), lambda i, j: (i, 0))
    return pl.pallas_call(
        functools.partial(_proj_kernel, n_q_blocks=q_cols // cb, n_rope_blocks=n_rope_blocks, n_main=n_main,
                          q_scale=(HEAD_DIM ** -0.5) * LOG2_E, iw_scale=(IDX_HEADS ** -0.5) * (IDX_DIM ** -0.5)),
        grid=(m // tm, n_main + 1),
        in_specs=[
            pl.BlockSpec((tm, d), lambda i, j: (i, 0)),
            pl.BlockSpec((1, d), lambda i, j: (0, 0)),
            pl.BlockSpec((d, cb), lambda i, j: (0, jnp.minimum(j, n_main - 1))),
            pl.BlockSpec((d, LANES), lambda i, j: (0, 0)),
            tab, tab, tab, tab,
        ],
        out_specs=[
            pl.BlockSpec((tm, cb), lambda i, j: (i, jnp.minimum(j, n_main - 1))),
            tail, tail, tail,
        ],
        out_shape=[
            jax.ShapeDtypeStruct((m, main_cols), BF16),
            jax.ShapeDtypeStruct((m, LANES), BF16),
            jax.ShapeDtypeStruct((m, LANES), BF16),
            jax.ShapeDtypeStruct((m, LANES), F32),
        ],
        scratch_shapes=[pltpu.VMEM((tm, d), BF16)],
        compiler_params=_params("parallel", "arbitrary"),
        name="dsa_project",
    )(x, g.reshape(1, d), w_main, w_tail, c128, s128, c64, s64)


def _dsa_kernel(q_ref, iq_ref, k_ref, v_ref, ike_ref, iko_ref, iw_ref, o_ref,
                keyt_ref, key16_ref, qst_ref, iwt_ref, m_ref, acc_ref, *, top_k, n_heads, kb):
    i = pl.program_id(1)
    t = q_ref.shape[1]
    seq_len = k_ref.shape[1]
    rep = n_heads // N_KV_HEADS
    n_blocks = (i * t) // kb + 1
    idx_bits = (seq_len - 1).bit_length()

    for h in range(n_heads):
        qst_ref[h * t:(h + 1) * t, :] = q_ref[0, :, h * HEAD_DIM:(h + 1) * HEAD_DIM]
    iwt_ref[...] = iw_ref[0].T

    krow_local = lax.broadcasted_iota(I32, (kb, t), 0)
    qpos = i * t + lax.broadcasted_iota(I32, (1, t), 1)

    def score_block(b, carry):
        s0 = pl.multiple_of(b * kb, kb)
        ike = ike_ref[0, pl.ds(s0, kb), :]
        iko = iko_ref[0, pl.ds(s0, kb), :]
        dots = []
        for p in range(IDX_HEADS // 2):
            iqp = iq_ref[0, :, p * LANES:(p + 1) * LANES]
            dots += [_dot_nt(ike, iqp), _dot_nt(iko, iqp)]
        sc = jnp.zeros((kb, t), F32)
        for h, d in enumerate(dots):
            sc = sc + iwt_ref[IDX_DIM + h:IDX_DIM + h + 1, :] * jnp.maximum(d, 0.0)
        bits = lax.bitcast_convert_type(sc, I32)
        key = jnp.where(bits < 0, bits ^ 0x7FFFFFFF, bits)
        key = jnp.where(s0 + krow_local <= qpos, key, INT_MIN)
        keyt_ref[pl.ds(s0, kb), :] = key
        key16_ref[pl.ds(s0, kb), :] = jnp.right_shift(key, 16).astype(I16)
        return carry

    lax.fori_loop(0, n_blocks, score_block, 0)

    def count_where(pred):
        def body(b, cnt):
            s0 = pl.multiple_of(b * kb, kb)
            hit = pred(keyt_ref[pl.ds(s0, kb), :], s0 + krow_local).astype(I32)
            part = jnp.sum(hit.reshape(COUNT_CHAINS, kb // (COUNT_CHAINS * SUBLANES), SUBLANES, t), axis=1)
            return cnt + jnp.sum(part, axis=0)

        cnt = lax.fori_loop(0, n_blocks, body, jnp.zeros((SUBLANES, t), I32))
        return jnp.sum(cnt, axis=0, keepdims=True)

    def count_ge(cand):
        return count_where(lambda kk, krow: kk >= cand)

    rows16 = 2 * SUBLANES

    def count16(pred):
        def body(b, cnt):
            s0 = pl.multiple_of(b * kb, kb)
            hit = pred(key16_ref[pl.ds(s0, kb), :]).astype(I16)
            tiles = [hit[r:r + rows16] for r in range(0, kb, rows16)]
            chains = [functools.reduce(lax.add, tiles[c::COUNT_CHAINS]) for c in range(COUNT_CHAINS)]
            return cnt + functools.reduce(lax.add, chains)

        cnt = lax.fori_loop(0, n_blocks, body, jnp.zeros((rows16, t), I16))
        return jnp.sum(cnt.astype(I32), axis=0, keepdims=True)

    def count_ge_hi(cand):
        cand_hi = jnp.right_shift(cand, 16).astype(I16)
        return count16(lambda k16: k16 >= cand_hi)

    c0 = count_ge_hi(jnp.zeros((1, t), I32))
    tau = jnp.where(c0 >= top_k, 0, INT_MIN).astype(I32)
    n_sel = jnp.where(c0 >= top_k, c0, 0)

    def n_pending(n_sel):
        return jnp.max(jnp.where((qpos >= top_k) & (n_sel != top_k), 1, 0))

    def bit_search(counter, first, last, carry):
        def cond(c):
            return (c[0] < last) & (c[3] > 0)

        def body(c):
            bi, tau, n_sel, _ = c
            for _ in range(BIT_STEPS_PER_CHECK):
                cand = tau | jnp.left_shift(jnp.int32(1), jnp.maximum(30 - bi, 0))
                n = counter(cand)
                keep = (n >= top_k) & (bi < last)
                tau = jnp.where(keep, cand, tau)
                n_sel = jnp.where(keep, n, n_sel)
                bi = bi + 1
            return bi, tau, n_sel, n_pending(n_sel)

        _, tau, n_sel, pending = lax.while_loop(cond, body, (jnp.int32(first),) + carry)
        return tau, n_sel, pending

    carry = bit_search(count_ge_hi, 0, HI_BIT_STEPS, (tau, n_sel, n_pending(n_sel)))

    tau_hi = jnp.right_shift(carry[0], 16)
    tau_hi16 = tau_hi.astype(I16)
    n_above = count16(lambda k16: k16 > tau_hi16)

    @pl.when(carry[2] > 0)
    def _():
        def build_low(b, c):
            s0 = pl.multiple_of(b * kb, kb)
            kk = keyt_ref[pl.ds(s0, kb), :]
            low = (kk & 0xFFFF) - 32768
            key16_ref[pl.ds(s0, kb), :] = jnp.where(jnp.right_shift(kk, 16) == tau_hi, low, -32768).astype(I16)
            return c

        lax.fori_loop(0, n_blocks, build_low, 0)

    def count_ge_lo(cand):
        cand_lo = ((cand & 0xFFFF) - 32768).astype(I16)
        return n_above + count16(lambda k16: k16 >= cand_lo)

    tau, n_sel, _ = bit_search(count_ge_lo, HI_BIT_STEPS, 31, carry)

    @pl.when(jnp.max(n_sel) > top_k)
    def _():
        need = top_k - count_ge(tau + 1)

        def idx_body(bi, cut):
            cand = cut | jnp.left_shift(jnp.int32(1), idx_bits - 1 - bi)
            n = count_where(lambda kk, krow: (kk == tau) & (krow < cand))
            return jnp.where(n < need, cand, cut)

        cut = lax.fori_loop(0, idx_bits, idx_body, jnp.zeros((1, t), I32))
        cut = jnp.where(n_sel > top_k, cut, seq_len)

        def demote(b, carry):
            s0 = pl.multiple_of(b * kb, kb)
            kk = keyt_ref[pl.ds(s0, kb), :]
            keyt_ref[pl.ds(s0, kb), :] = jnp.where((kk == tau) & (s0 + krow_local > cut), tau - 1, kk)
            return carry

        lax.fori_loop(0, n_blocks, demote, 0)

    tau = jnp.maximum(tau, INT_MIN + 1)

    m_ref[...] = jnp.full(m_ref.shape, MASKED_SCORE, F32)
    acc_ref[...] = jnp.zeros(acc_ref.shape, F32)
    ones = jnp.ones((kb, HEAD_DIM), BF16)

    def att_block(b, carry):
        s0 = pl.multiple_of(b * kb, kb)
        neg = jnp.where(keyt_ref[pl.ds(s0, kb), :] >= tau, 0.0, MASKED_SCORE).T
        logits = [_dot_nt(qst_ref[g * rep * t:(g + 1) * rep * t, :],
                          k_ref[0, pl.ds(s0, kb), g * HEAD_DIM:(g + 1) * HEAD_DIM]) for g in range(N_KV_HEADS)]
        for g in range(N_KV_HEADS):
            gs = slice(g * HEAD_DIM, (g + 1) * HEAD_DIM)
            s = logits[g]
            v1 = jnp.concatenate([v_ref[0, pl.ds(s0, kb), gs], ones], axis=1)
            ps = []
            alphas = []
            for r in range(rep):
                hrows = slice((g * rep + r) * t, (g * rep + r + 1) * t)
                sr = s[r * t:(r + 1) * t] + neg
                tiles = [sr[:, c * LANES:(c + 1) * LANES] for c in range(kb // LANES)]
                mx = functools.reduce(jnp.maximum, tiles)
                m_old = m_ref[hrows, :]
                m_new = jnp.maximum(m_old, jnp.max(mx, axis=1, keepdims=True))
                m_ref[hrows, :] = m_new
                alphas.append(jnp.exp2(m_old - m_new))
                ps.append(jnp.concatenate([jnp.exp2(x - m_new) for x in tiles], axis=1).astype(BF16))
            pv = _dot(jnp.concatenate(ps, axis=0), v1)
            for r in range(rep):
                hrows = slice((g * rep + r) * t, (g * rep + r + 1) * t)
                for c in range(2):
                    cs = slice(c * HEAD_DIM, (c + 1) * HEAD_DIM)
                    acc_ref[hrows, cs] = alphas[r] * acc_ref[hrows, cs] + pv[r * t:(r + 1) * t, cs]
        return carry

    lax.fori_loop(0, n_blocks, att_block, 0)

    for h in range(n_heads):
        hrows = slice(h * t, (h + 1) * t)
        o = acc_ref[hrows, 0:HEAD_DIM] / acc_ref[hrows, HEAD_DIM:2 * HEAD_DIM]
        o_ref[0, :, h * HEAD_DIM:(h + 1) * HEAD_DIM] = o.astype(BF16)


def _dsa_attend(p, ike, iko, iw, batch, seq_len, d):
    n_heads = d // HEAD_DIM
    kv_cols = N_KV_HEADS * HEAD_DIM
    iq_cols = IDX_HEADS * IDX_DIM
    t = min(DSA_TILE, seq_len)
    kb = min(DSA_KEY_BLOCK, seq_len)
    top_k = min(TOP_K_MAX, seq_len // 4)
    assert (d + 2 * kv_cols) % iq_cols == 0 and d % kv_cols == 0 and t % LANES == 0 and kb % t == 0
    p3 = p.reshape(batch, seq_len, p.shape[-1])
    whole = functools.partial(pl.BlockSpec, pipeline_mode=pl.Buffered(1))
    out = pl.pallas_call(
        functools.partial(_dsa_kernel, top_k=top_k, n_heads=n_heads, kb=kb),
        grid=(batch, seq_len // t),
        in_specs=[
            pl.BlockSpec((1, t, d), lambda b, i: (b, i, 0)),
            pl.BlockSpec((1, t, iq_cols), lambda b, i: (b, i, (d + 2 * kv_cols) // iq_cols)),
            whole((1, seq_len, kv_cols), lambda b, i: (b, 0, d // kv_cols)),
            whole((1, seq_len, kv_cols), lambda b, i: (b, 0, d // kv_cols + 1)),
            whole((1, seq_len, LANES), lambda b, i: (b, 0, 0)),
            whole((1, seq_len, LANES), lambda b, i: (b, 0, 0)),
            pl.BlockSpec((1, t, LANES), lambda b, i: (b, i, 0)),
        ],
        out_specs=pl.BlockSpec((1, t, d), lambda b, i: (b, i, 0)),
        out_shape=jax.ShapeDtypeStruct((batch, seq_len, d), BF16),
        scratch_shapes=[
            pltpu.VMEM((seq_len, t), I32),
            pltpu.VMEM((seq_len, t), I16),
            pltpu.VMEM((n_heads * t, HEAD_DIM), BF16),
            pltpu.VMEM((LANES, t), F32),
            pltpu.VMEM((n_heads * t, LANES), F32),
            pltpu.VMEM((n_heads * t, 2 * HEAD_DIM), F32),
        ],
        compiler_params=_params("parallel", "arbitrary"),
        name="dsa_attend",
    )(p3, p3, p3, p3,
      ike.reshape(batch, seq_len, LANES), iko.reshape(batch, seq_len, LANES),
      iw.reshape(batch, seq_len, LANES))
    return out.reshape(batch * seq_len, d)


def _proj_res_kernel(a_ref, w_ref, b_ref, x_ref, o_ref):
    o_ref[...] = x_ref[...] + _dot(a_ref[...], w_ref[...]) + b_ref[...]


def _proj_residual(a, w, b, x):
    m, d = x.shape
    kdim = a.shape[1]
    tm = min(ROW_TILE, m)
    return pl.pallas_call(
        _proj_res_kernel,
        grid=(m // tm,),
        in_specs=[
            pl.BlockSpec((tm, kdim), lambda i: (i, 0)),
            pl.BlockSpec((kdim, d), lambda i: (0, 0)),
            pl.BlockSpec((1, d), lambda i: (0, 0)),
            pl.BlockSpec((tm, d), lambda i: (i, 0)),
        ],
        out_specs=pl.BlockSpec((tm, d), lambda i: (i, 0)),
        out_shape=jax.ShapeDtypeStruct((m, d), F32),
        compiler_params=_params("parallel"),
        name="proj_residual",
    )(a, w, b.reshape(1, d), x)


def _glu_kernel(x_ref, g_ref, wa_ref, wg_ref, ba_ref, bg_ref, o_ref, h_ref):
    @pl.when(pl.program_id(1) == 0)
    def _():
        h_ref[...] = _rms(x_ref[...], g_ref[...]).astype(BF16)

    tm = h_ref.shape[0]
    rc = min(PROJ_ROW_CHUNK, tm)
    for r0 in range(0, tm, rc):
        h = h_ref[r0:r0 + rc, :]
        a = _dot(h, wa_ref[...]) + ba_ref[...]
        gate = _dot(h, wg_ref[...]) + bg_ref[...]
        o_ref[r0:r0 + rc, :] = a * jax.nn.sigmoid(gate)


def _glu(x, g, w, b):
    m, d = x.shape
    tm, tn = min(MLP_ROW_TILE, m), min(COL_TILE, d)
    nb = d // tn
    b2 = b.reshape(1, 2 * d)
    return pl.pallas_call(
        _glu_kernel,
        grid=(m // tm, nb),
        in_specs=[
            pl.BlockSpec((tm, d), lambda i, j: (i, 0)),
            pl.BlockSpec((1, d), lambda i, j: (0, 0)),
            pl.BlockSpec((d, tn), lambda i, j: (0, j)),
            pl.BlockSpec((d, tn), lambda i, j: (0, j + nb)),
            pl.BlockSpec((1, tn), lambda i, j: (0, j)),
            pl.BlockSpec((1, tn), lambda i, j: (0, j + nb)),
        ],
        out_specs=pl.BlockSpec((tm, tn), lambda i, j: (i, j)),
        out_shape=jax.ShapeDtypeStruct((m, d), F32),
        scratch_shapes=[pltpu.VMEM((tm, d), BF16)],
        compiler_params=_params("parallel", "arbitrary"),
        name="conv_glu",
    )(x, g.reshape(1, d), w, w, b2, b2)


def _dwconv_kernel(u_ref, uh_ref, w_ref, b_ref, lg_ref, lb_ref, o_ref, sh_ref, c_ref, *, tiles_per_seq):
    i = pl.program_id(0)
    tm, d = u_ref.shape
    rows = tm + CONV_HALO
    halo = jnp.where(i % tiles_per_seq == 0, 0.0, uh_ref[...])
    off = CONV_HALO - (CONV_WIDTH - 1)
    cw = sh_ref.shape[2]
    rc = min(CONV_ROW_CHUNK, tm)
    for c0 in range(0, d, cw):
        cols = slice(c0, c0 + cw)
        ext = jnp.concatenate([halo[:, cols], u_ref[:, cols]], axis=0)
        sh_ref[0] = ext
        for j in range(1, SUBLANES):
            sh_ref[j] = pltpu.roll(ext, rows - j, 0)
        for r0 in range(0, tm, rc):
            acc = jnp.zeros((rc, cw), F32)
            for k in range(CONV_WIDTH):
                j = (off + k) % SUBLANES
                a = off + k - j + r0
                acc = acc + sh_ref[j, a:a + rc, :] * w_ref[k:k + 1, cols]
            c_ref[r0:r0 + rc, cols] = acc + b_ref[:, cols]
    c = c_ref[...]
    mu = jnp.mean(c, axis=-1, keepdims=True)
    cc = c - mu
    var = jnp.mean(cc * cc, axis=-1, keepdims=True)
    y = cc * lax.rsqrt(var + EPS) * lg_ref[...] + lb_ref[...]
    o_ref[...] = (y * jax.nn.sigmoid(y)).astype(BF16)


def _dwconv(u, w_dw, b_dw, ln_g, ln_b, seq_len):
    m, d = u.shape
    tm = min(CONV_ROW_TILE, seq_len)
    halo_blocks = tm // CONV_HALO
    row = lambda v: v.reshape(1, d)
    return pl.pallas_call(
        functools.partial(_dwconv_kernel, tiles_per_seq=seq_len // tm),
        grid=(m // tm,),
        in_specs=[
            pl.BlockSpec((tm, d), lambda i: (i, 0)),
            pl.BlockSpec((CONV_HALO, d), lambda i: (jnp.maximum(i * halo_blocks - 1, 0), 0)),
            pl.BlockSpec((CONV_WIDTH, d), lambda i: (0, 0)),
            pl.BlockSpec((1, d), lambda i: (0, 0)),
            pl.BlockSpec((1, d), lambda i: (0, 0)),
            pl.BlockSpec((1, d), lambda i: (0, 0)),
        ],
        out_specs=pl.BlockSpec((tm, d), lambda i: (i, 0)),
        out_shape=jax.ShapeDtypeStruct((m, d), BF16),
        scratch_shapes=[pltpu.VMEM((SUBLANES, tm + CONV_HALO, min(COL_TILE, d)), F32), pltpu.VMEM((tm, d), F32)],
        compiler_params=_params("parallel"),
        name="conv_dw_ln",
    )(u, u, w_dw, row(b_dw), row(ln_g), row(ln_b))


def kernel(x, norm_mix, norm_mlp, mlp_up, mlp_down, pool_w, pool_scale, dsa_w_in, dsa_w_out,
           conv_w_pw1, conv_b_pw1, conv_w_dw, conv_b_dw, conv_ln_g, conv_ln_b, conv_w_pw2,
           conv_b_pw2, norm_final):
    batch, seq_len, d = x.shape
    depth = norm_mix.shape[0]
    xf = x.reshape(batch * seq_len, d)
    up_bf16, down_bf16 = mlp_up.astype(BF16), mlp_down.astype(BF16)
    for i in range(depth):
        kind, j = i % N_MIXERS, i // N_MIXERS
        if kind == 0:
            xf = _pool_layer(xf, norm_mix[i], pool_w[j].astype(BF16), pool_scale[j], seq_len)
        elif kind == 1:
            p, ike, iko, iw = _dsa_project(xf, norm_mix[i], dsa_w_in[j], seq_len)
            o = _dsa_attend(p, ike, iko, iw, batch, seq_len, d)
            xf = _proj_residual(o, dsa_w_out[j].astype(BF16), jnp.zeros((d,), F32), xf)
        else:
            u = _glu(xf, norm_mix[i], conv_w_pw1[j].astype(BF16), conv_b_pw1[j])
            u = _dwconv(u, conv_w_dw[j], conv_b_dw[j], conv_ln_g[j], conv_ln_b[j], seq_len)
            xf = _proj_residual(u, conv_w_pw2[j].astype(BF16), conv_b_pw2[j], xf)
        xf = _mlp(xf, norm_mlp[i], up_bf16, down_bf16, i, norm_final, final_norm=(i == depth - 1))
    return xf.reshape(batch, seq_len, d)
```
